```python
import math
import jax, jax.numpy as jnp
from jax import lax
import numpy as np

D_MODEL = 4096
BATCH = 8
SEQ = 2048
DEPTH = 1
DEC_BATCH = 32
DEC_SEQ = 64
PAST_LEN = 2048

CHUNK = 64
N_HEADS = 8
HEAD_DIM = 128
V_DIM = 2 * HEAD_DIM
QK_W = N_HEADS * 2 * HEAD_DIM
ATTN_W = N_HEADS * V_DIM
POOL_WINDOWS = (2, 4, 8, 16)
POOL_GROUPS = len(POOL_WINDOWS)
POOL_W = D_MODEL // 2
POOL_GROUP_W = POOL_W // POOL_GROUPS
POOL_HIST = max(POOL_WINDOWS) - 1
D_FF = 4 * D_MODEL
IN_COLS = 2 * QK_W + ATTN_W + POOL_W + 2 * D_MODEL
Q_BLOCK = 128
NORM_EPS = 1e-6
SUBLN_EPS = 1e-5
NEG_INF = -1e30

kernel_name = 'streaming_diffattn_pool_hybrid_step'


def rmsnorm(x, g, eps=NORM_EPS):
    xf = x.astype(jnp.float32)
    y = xf * lax.rsqrt(jnp.mean(xf * xf, axis=-1, keepdims=True) + eps)
    return (y * g.astype(jnp.float32)).astype(x.dtype)


def diff_attend(q, pos_q, parts, lam):
    logits = jnp.concatenate(
        [jnp.einsum('bqhcd,bkhcd->cbhqk', q, k, preferred_element_type=jnp.float32) for k, _, _ in parts],
        axis=-1) * (HEAD_DIM ** -0.5)
    pos_k = jnp.concatenate([pk for _, _, pk in parts])
    mask = (pos_k[None, :] // CHUNK) <= (pos_q[:, None] // CHUNK)
    p = jax.nn.softmax(jnp.where(mask, logits, NEG_INF), axis=-1)
    w = p[0] - lam * p[1]
    terms = []
    start = 0
    for k, v, _ in parts:
        n = k.shape[1]
        terms.append(jnp.einsum('bhqk,bkhe->bqhe', w[..., start:start + n], v.astype(jnp.float32)))
        start += n
    return sum(terms[1:], terms[0])


def prompt_attend(q, k, v, pos, lam):
    B, T = q.shape[:2]
    nb = T // Q_BLOCK
    qb = q.reshape(B, nb, Q_BLOCK, N_HEADS, 2, HEAD_DIM).swapaxes(0, 1)
    pb = pos.reshape(nb, Q_BLOCK)
    ob = lax.map(lambda a: diff_attend(a[0], a[1], [(k, v, pos)], lam), (qb, pb))
    return ob.swapaxes(0, 1).reshape(B, T, N_HEADS, V_DIM)


def multiscale_pool(u_pad, pos):
    B, L, _ = u_pad.shape
    T = L - POOL_HIST
    uf = u_pad.astype(jnp.float32)
    csum = jnp.concatenate([jnp.zeros((B, 1, POOL_W), jnp.float32), jnp.cumsum(uf, axis=1)], axis=1)
    groups = []
    for g, w in enumerate(POOL_WINDOWS):
        c0 = g * POOL_GROUP_W
        c1 = c0 + POOL_GROUP_W
        win_sum = csum[:, POOL_HIST + 1:, c0:c1] - csum[:, POOL_HIST + 1 - w:POOL_HIST + 1 - w + T, c0:c1]
        count = jnp.minimum(pos + 1, w).astype(jnp.float32)[None, :, None]
        groups.append(win_sum / count - uf[:, POOL_HIST:, c0:c1])
    return jnp.stack(groups, axis=2)


def trunk_layer(x, pos, pool_hist, kv_cache, lam_init, g_norm1, w_in, g_q, g_k, lambda_q1, lambda_k1,
                lambda_q2, lambda_k2, g_subln, w_attn_out, w_pool, pool_scale, w_pool_out, w_o,
                g_norm2, w_up, w_down):
    B, T, _ = x.shape
    xn = rmsnorm(x, g_norm1)
    h = xn @ w_in
    o1 = QK_W
    o2 = o1 + QK_W
    o3 = o2 + ATTN_W
    o4 = o3 + POOL_W
    o5 = o4 + D_MODEL
    q, k, v, u, ga, gb = jnp.split(h, [o1, o2, o3, o4, o5], axis=-1)
    q = rmsnorm(q.reshape(B, T, N_HEADS, 2, HEAD_DIM), g_q)
    k = rmsnorm(k.reshape(B, T, N_HEADS, 2, HEAD_DIM), g_k)
    v = v.reshape(B, T, N_HEADS, V_DIM)
    lam = (jnp.exp(jnp.sum(lambda_q1.astype(jnp.float32) * lambda_k1.astype(jnp.float32)))
           - jnp.exp(jnp.sum(lambda_q2.astype(jnp.float32) * lambda_k2.astype(jnp.float32))) + lam_init)
    if kv_cache is None:
        o = prompt_attend(q, k, v, pos, lam)
    else:
        ck, cv = kv_cache
        pos_c = jnp.arange(ck.shape[1])
        o = diff_attend(q, pos, [(ck, cv, pos_c), (k, v, pos)], lam)
    o = rmsnorm(o, g_subln, SUBLN_EPS) * (1.0 - lam_init)
    y_a = o.reshape(B, T, ATTN_W).astype(x.dtype) @ w_attn_out
    u_pad = jnp.concatenate([pool_hist.astype(u.dtype), u], axis=1)
    pooled = multiscale_pool(u_pad, pos)
    mixed = jnp.einsum('btgc,gcd->btgd', pooled, w_pool.astype(jnp.float32)).reshape(B, T, POOL_W)
    y_b = (mixed * pool_scale.astype(jnp.float32)).astype(x.dtype) @ w_pool_out
    merged = jax.nn.sigmoid(ga) * y_a + jax.nn.sigmoid(gb) * y_b
    x = x + merged @ w_o
    hn = rmsnorm(x, g_norm2)
    x = x + jnp.square(jax.nn.relu(hn @ w_up)) @ w_down
    return x, k, v, u_pad[:, -POOL_HIST:]


def setup_inputs(seed: int = 0) -> dict:
    key = jax.random.key(seed)
    ks = jax.random.split(key, 24)

    def nrm(k, shape, scale):
        return jax.random.normal(k, shape, jnp.float32) * scale

    def gain(k, shape, s=0.05):
        return 1.0 + nrm(k, shape, s)

    return {
        'x_prompt': nrm(ks[0], (BATCH, SEQ, D_MODEL), 1.0),
        'x_sample': nrm(ks[1], (DEC_BATCH, DEC_SEQ, D_MODEL), 1.0),
        'cache_k': nrm(ks[2], (DEPTH, DEC_BATCH, PAST_LEN, N_HEADS, 2, HEAD_DIM), 1.0),
        'cache_v': nrm(ks[3], (DEPTH, DEC_BATCH, PAST_LEN, N_HEADS, V_DIM), 1.0),
        'state_pool': nrm(ks[4], (DEPTH, DEC_BATCH, POOL_HIST, POOL_W), 1.0),
        'g_norm1': gain(ks[5], (DEPTH, D_MODEL)),
        'w_in': nrm(ks[6], (DEPTH, D_MODEL, IN_COLS), D_MODEL ** -0.5),
        'g_q': gain(ks[7], (DEPTH, HEAD_DIM)),
        'g_k': gain(ks[8], (DEPTH, HEAD_DIM)),
        'lambda_q1': nrm(ks[9], (DEPTH, HEAD_DIM), 0.1),
        'lambda_k1': nrm(ks[10], (DEPTH, HEAD_DIM), 0.1),
        'lambda_q2': nrm(ks[11], (DEPTH, HEAD_DIM), 0.1),
        'lambda_k2': nrm(ks[12], (DEPTH, HEAD_DIM), 0.1),
        'g_subln': gain(ks[13], (DEPTH, V_DIM)),
        'w_attn_out': nrm(ks[14], (DEPTH, ATTN_W, D_MODEL), ATTN_W ** -0.5),
        'w_pool': nrm(ks[15], (DEPTH, POOL_GROUPS, POOL_GROUP_W, POOL_GROUP_W), POOL_GROUP_W ** -0.5),
        'pool_scale': gain(ks[16], (DEPTH, POOL_W), 0.1),
        'w_pool_out': nrm(ks[17], (DEPTH, POOL_W, D_MODEL), POOL_W ** -0.5),
        'w_o': nrm(ks[18], (DEPTH, D_MODEL, D_MODEL), D_MODEL ** -0.5),
        'g_norm2': gain(ks[19], (DEPTH, D_MODEL)),
        'w_up': nrm(ks[20], (DEPTH, D_MODEL, D_FF), D_MODEL ** -0.5),
        'w_down': nrm(ks[21], (DEPTH, D_FF, D_MODEL), D_FF ** -0.5),
    }


def reference(x_prompt, x_sample, cache_k, cache_v, state_pool, g_norm1, w_in, g_q, g_k, lambda_q1,
              lambda_k1, lambda_q2, lambda_k2, g_subln, w_attn_out, w_pool, pool_scale, w_pool_out, w_o,
              g_norm2, w_up, w_down):
    past = cache_k.shape[2]
    pos_p = jnp.arange(x_prompt.shape[1])
    pos_s = past + jnp.arange(x_sample.shape[1])
    hist_p = jnp.zeros((x_prompt.shape[0], POOL_HIST, POOL_W), x_prompt.dtype)
    y_p, y_s = x_prompt, x_sample
    kp_l, vp_l, sp_l, ks_l, vs_l, ss_l = [], [], [], [], [], []
    for l in range(DEPTH):
        lam_init = 0.8 - 0.6 * math.exp(-0.3 * l)
        w = (g_norm1[l], w_in[l], g_q[l], g_k[l], lambda_q1[l], lambda_k1[l], lambda_q2[l], lambda_k2[l],
             g_subln[l], w_attn_out[l], w_pool[l], pool_scale[l], w_pool_out[l], w_o[l], g_norm2[l],
             w_up[l], w_down[l])
        y_p, kp, vp, sp = trunk_layer(y_p, pos_p, hist_p, None, lam_init, *w)
        y_s, ksm, vsm, ssm = trunk_layer(y_s, pos_s, state_pool[l], (cache_k[l], cache_v[l]), lam_init, *w)
        kp_l.append(kp)
        vp_l.append(vp)
        sp_l.append(sp)
        ks_l.append(ksm)
        vs_l.append(vsm)
        ss_l.append(ssm)
    return (y_p, y_s, jnp.stack(kp_l), jnp.stack(vp_l), jnp.stack(sp_l),
            jnp.stack(ks_l), jnp.stack(vs_l), jnp.stack(ss_l))
```

```python
import functools
import math

import jax
import jax.numpy as jnp
from jax import lax
from jax.experimental import pallas as pl
from jax.experimental.pallas import tpu as pltpu

CHUNK = 64
POOL_WINDOWS = (2, 4, 8, 16)
POOL_HIST = max(POOL_WINDOWS) - 1
HIST_ROWS = 16
NORM_EPS = 1e-6
SUBLN_EPS = 1e-5
NEG_INF = -1e30
V7X_VMEM_LIMIT_BYTES = 56 * 1024 * 1024

F32 = jnp.float32
BF16 = jnp.bfloat16


def _tile(dim, pref):
    t = min(dim, pref)
    while dim % t:
        t -= 1
    return t


def _params(*sem):
    return pltpu.CompilerParams(dimension_semantics=sem, vmem_limit_bytes=V7X_VMEM_LIMIT_BYTES)


def _dot(a, b):
    return jnp.dot(a, b, preferred_element_type=F32)


def _dot_nt(a, b):
    return lax.dot_general(a, b, (((1,), (1,)), ((), ())), preferred_element_type=F32)


def _rmsnorm_cast_kernel(x_ref, g_ref, o_ref, *, eps):
    x = x_ref[...]
    ms = jnp.mean(x * x, axis=-1, keepdims=True)
    o_ref[...] = (x * lax.rsqrt(ms + eps) * g_ref[...]).astype(o_ref.dtype)


def _rmsnorm_cast(x, g, eps):
    m, d = x.shape
    tm = _tile(m, 256)
    return pl.pallas_call(
        functools.partial(_rmsnorm_cast_kernel, eps=eps),
        grid=(m // tm,),
        in_specs=[pl.BlockSpec((tm, d), lambda i: (i, 0)), pl.BlockSpec((1, d), lambda i: (0, 0))],
        out_specs=pl.BlockSpec((tm, d), lambda i: (i, 0)),
        out_shape=jax.ShapeDtypeStruct((m, d), BF16),
        compiler_params=_params("parallel"),
        name="rmsnorm_cast",
    )(x, g.reshape(1, d).astype(F32))


def _tiled_call(body, rows, weights, tiles, vecs, out_dtypes, n_cols, tm, tn, name):
    m = rows[0].shape[0]
    tm = _tile(m, tm)
    tn = _tile(n_cols, tn)
    in_specs, args = [], []
    for a in rows:
        in_specs.append(pl.BlockSpec((tm, a.shape[1]), lambda i, j: (i, 0)))
        args.append(a)
    for w, c0 in weights:
        assert c0 % tn == 0
        in_specs.append(pl.BlockSpec((w.shape[0], tn), lambda i, j, o=c0 // tn: (0, j + o)))
        args.append(w)
    for t, c0 in tiles:
        assert c0 % tn == 0
        in_specs.append(pl.BlockSpec((tm, tn), lambda i, j, o=c0 // tn: (i, j + o)))
        args.append(t)
    for v in vecs:
        in_specs.append(pl.BlockSpec(v.shape, lambda i, j: (0, 0)))
        args.append(v)
    out_specs = [pl.BlockSpec((tm, tn), lambda i, j: (i, j)) for _ in out_dtypes]
    out_shape = [jax.ShapeDtypeStruct((m, n_cols), dt) for dt in out_dtypes]
    return pl.pallas_call(
        body,
        grid=(m // tm, n_cols // tn),
        in_specs=in_specs,
        out_specs=out_specs,
        out_shape=out_shape,
        compiler_params=_params("parallel", "parallel"),
        name=name,
    )(*args)


def _head_rmsnorm_store(acc, g, eps, scale, out_refs):
    group = g.shape[1]
    for c in range(acc.shape[1] // group):
        sl = slice(c * group, (c + 1) * group)
        blk = acc[:, sl]
        ms = jnp.mean(blk * blk, axis=-1, keepdims=True)
        y = blk * lax.rsqrt(ms + eps) * g
        for ref, s in zip(out_refs, scale):
            ref[:, sl] = (y if s == 1.0 else y * s).astype(ref.dtype)


def _q_body(a_ref, w_ref, g_ref, q_ref, *, scale):
    _head_rmsnorm_store(_dot(a_ref[...], w_ref[...]), g_ref[...], NORM_EPS, (scale,), (q_ref,))


def _k_body(a_ref, w_ref, g_ref, kf_ref, kb_ref):
    _head_rmsnorm_store(_dot(a_ref[...], w_ref[...]), g_ref[...], NORM_EPS, (1.0, 1.0), (kf_ref, kb_ref))


def _v_body(a_ref, w_ref, vf_ref, vb_ref):
    acc = _dot(a_ref[...], w_ref[...])
    vf_ref[...] = acc
    vb_ref[...] = acc.astype(vb_ref.dtype)


def _u_body(a_ref, w_ref, u_ref):
    u_ref[...] = _dot(a_ref[...], w_ref[...])


def _gate_body(a_ref, w_ref, g_ref):
    g_ref[...] = jax.nn.sigmoid(_dot(a_ref[...], w_ref[...])).astype(g_ref.dtype)


def _merge_body(o_ref, yb_ref, wa_ref, wp_ref, ga_ref, gb_ref, out_ref):
    ya = _dot(o_ref[...], wa_ref[...])
    yb = _dot(yb_ref[...], wp_ref[...])
    out_ref[...] = (ga_ref[...].astype(F32) * ya + gb_ref[...].astype(F32) * yb).astype(out_ref.dtype)


def _residual_body(a_ref, w_ref, x_ref, out_ref):
    out_ref[...] = x_ref[...] + _dot(a_ref[...], w_ref[...])


def _relu2_body(a_ref, w_ref, out_ref):
    h = jnp.maximum(_dot(a_ref[...], w_ref[...]), 0.0)
    out_ref[...] = (h * h).astype(out_ref.dtype)


def _down_body(a_ref, w_ref, x_ref, out_ref):
    k = pl.program_id(2)
    part = _dot(a_ref[...], w_ref[...])

    @pl.when(k == 0)
    def _():
        out_ref[...] = x_ref[...] + part

    @pl.when(k > 0)
    def _():
        out_ref[...] += part


def _down_proj(a, w, x, tm, tn, tk):
    m, kdim = a.shape
    n = w.shape[1]
    tm, tn, tk = _tile(m, tm), _tile(n, tn), _tile(kdim, tk)
    return pl.pallas_call(
        _down_body,
        grid=(m // tm, n // tn, kdim // tk),
        in_specs=[
            pl.BlockSpec((tm, tk), lambda i, j, k: (i, k)),
            pl.BlockSpec((tk, tn), lambda i, j, k: (k, j)),
            pl.BlockSpec((tm, tn), lambda i, j, k: (i, j)),
        ],
        out_specs=pl.BlockSpec((tm, tn), lambda i, j, k: (i, j)),
        out_shape=jax.ShapeDtypeStruct((m, n), F32),
        compiler_params=_params("parallel", "parallel", "arbitrary"),
        name="down_proj",
    )(a, w, x)


def _online_softmax_step(idx, s, v, m_sc, l_sc, acc_sc):
    m_old = m_sc[idx]
    m_new = jnp.maximum(m_old, jnp.max(s, axis=-1, keepdims=True))
    alpha = jnp.exp(m_old - m_new)
    p = jnp.exp(s - m_new)
    l_sc[idx] = alpha * l_sc[idx] + jnp.sum(p, axis=-1, keepdims=True)
    acc_sc[idx] = alpha * acc_sc[idx] + _dot(p.astype(v.dtype), v)
    m_sc[idx] = m_new


def _lambda(lq1, lk1, lq2, lk2, lam_init):
    return (jnp.exp(jnp.sum(lq1[...] * lk1[...], keepdims=True))
            - jnp.exp(jnp.sum(lq2[...] * lk2[...], keepdims=True)) + lam_init)


def _diff_subln(idx, lam, gs, lam_init, l_sc, acc_sc):
    o = acc_sc[idx] / l_sc[idx] - lam * (acc_sc[idx + 1] / l_sc[idx + 1])
    ms = jnp.mean(o * o, axis=-1, keepdims=True)
    return o * lax.rsqrt(ms + SUBLN_EPS) * gs * (1.0 - lam_init)


def _attn_prompt_kernel(q_ref, k_ref, v_ref, lq1, lk1, lq2, lk2, gs_ref, o_ref, m_sc, l_sc, acc_sc,
                        *, tq, d, lam_init):
    qi = pl.program_id(2)
    m_sc[...] = jnp.full(m_sc.shape, -jnp.inf, F32)
    l_sc[...] = jnp.zeros(l_sc.shape, F32)
    acc_sc[...] = jnp.zeros(acc_sc.shape, F32)
    q = q_ref[...]

    def block(kb, mask):
        ks = pl.multiple_of(kb * tq, tq)
        kblk = k_ref[pl.ds(ks, tq), :]
        vblk = v_ref[pl.ds(ks, tq), :]
        for c in range(2):
            s = _dot_nt(q[:, c * d:(c + 1) * d], kblk[:, c * d:(c + 1) * d])
            if mask is not None:
                s = jnp.where(mask, s, NEG_INF)
            _online_softmax_step(c, s, vblk, m_sc, l_sc, acc_sc)

    def body(kb, carry):
        block(kb, None)
        return carry

    lax.fori_loop(0, qi, body, 0)
    rq = lax.broadcasted_iota(jnp.int32, (tq, tq), 0) // CHUNK
    rk = lax.broadcasted_iota(jnp.int32, (tq, tq), 1) // CHUNK
    block(qi, rk <= rq)
    lam = _lambda(lq1, lk1, lq2, lk2, lam_init)
    o_ref[...] = _diff_subln(0, lam, gs_ref[...], lam_init, l_sc, acc_sc).astype(o_ref.dtype)


def _attn_prompt(q, k, v, lam_vecs, g_subln, n_heads, lam_init):
    b, t, width = q.shape
    hw = width // n_heads
    d = hw // 2
    tq = _tile(t, 256)
    assert tq % CHUNK == 0
    vec = pl.BlockSpec((1, d), lambda bi, h, qi: (0, 0))
    return pl.pallas_call(
        functools.partial(_attn_prompt_kernel, tq=tq, d=d, lam_init=lam_init),
        grid=(b, n_heads, t // tq),
        in_specs=[
            pl.BlockSpec((None, tq, hw), lambda bi, h, qi: (bi, qi, h)),
            pl.BlockSpec((None, t, hw), lambda bi, h, qi: (bi, 0, h)),
            pl.BlockSpec((None, t, hw), lambda bi, h, qi: (bi, 0, h)),
            vec, vec, vec, vec,
            pl.BlockSpec((1, hw), lambda bi, h, qi: (0, 0)),
        ],
        out_specs=pl.BlockSpec((None, tq, hw), lambda bi, h, qi: (bi, qi, h)),
        out_shape=jax.ShapeDtypeStruct((b, t, width), BF16),
        scratch_shapes=[
            pltpu.VMEM((2, tq, 1), F32),
            pltpu.VMEM((2, tq, 1), F32),
            pltpu.VMEM((2, tq, hw), F32),
        ],
        compiler_params=_params("parallel", "parallel", "parallel"),
        name="attn_prompt",
    )(q, k, v, *lam_vecs, g_subln)


def _attn_sample_kernel(q_ref, ck_ref, cv_ref, kn_ref, vn_ref, lq1, lk1, lq2, lk2, gs_ref, o_ref,
                        m_sc, l_sc, acc_sc, *, n_heads, d, past, lam_init):
    kb = pl.program_id(1)
    nkb = pl.num_programs(1) - 1
    hw = 2 * d

    @pl.when(kb == 0)
    def _():
        m_sc[...] = jnp.full(m_sc.shape, -jnp.inf, F32)
        l_sc[...] = jnp.zeros(l_sc.shape, F32)
        acc_sc[...] = jnp.zeros(acc_sc.shape, F32)

    def update(k_of, v_of, mask):
        for h in range(n_heads):
            v = v_of(h)
            for c in range(2):
                c0 = h * hw + c * d
                s = _dot_nt(q_ref[:, c0:c0 + d], k_of(c0))
                if mask is not None:
                    s = jnp.where(mask, s, NEG_INF)
                _online_softmax_step(2 * h + c, s, v, m_sc, l_sc, acc_sc)

    @pl.when(kb < nkb)
    def _():
        update(lambda c0: ck_ref[:, c0:c0 + d].astype(BF16),
               lambda h: cv_ref[:, h * hw:(h + 1) * hw].astype(BF16), None)

    @pl.when(kb == nkb)
    def _():
        tq = q_ref.shape[0]
        rq = (past + lax.broadcasted_iota(jnp.int32, (tq, tq), 0)) // CHUNK
        rk = (past + lax.broadcasted_iota(jnp.int32, (tq, tq), 1)) // CHUNK
        update(lambda c0: kn_ref[:, c0:c0 + d], lambda h: vn_ref[:, h * hw:(h + 1) * hw], rk <= rq)
        lam = _lambda(lq1, lk1, lq2, lk2, lam_init)
        for h in range(n_heads):
            o = _diff_subln(2 * h, lam, gs_ref[...], lam_init, l_sc, acc_sc)
            o_ref[:, h * hw:(h + 1) * hw] = o.astype(o_ref.dtype)


def _attn_sample(q, cache_k, cache_v, k_new, v_new, lam_vecs, g_subln, n_heads, lam_init):
    b, tq, width = q.shape
    past = cache_k.shape[1]
    hw = width // n_heads
    d = hw // 2
    tk = _tile(past, 512)
    nkb = past // tk
    vec = pl.BlockSpec((1, d), lambda bi, kb: (0, 0))
    new = pl.BlockSpec((None, tq, width), lambda bi, kb: (bi, 0, 0))
    cache = pl.BlockSpec((None, tk, width), lambda bi, kb: (bi, jnp.minimum(kb, nkb - 1), 0))
    return pl.pallas_call(
        functools.partial(_attn_sample_kernel, n_heads=n_heads, d=d, past=past, lam_init=lam_init),
        grid=(b, nkb + 1),
        in_specs=[new, cache, cache, new, new, vec, vec, vec, vec,
                  pl.BlockSpec((1, hw), lambda bi, kb: (0, 0))],
        out_specs=new,
        out_shape=jax.ShapeDtypeStruct((b, tq, width), BF16),
        scratch_shapes=[
            pltpu.VMEM((2 * n_heads, tq, 1), F32),
            pltpu.VMEM((2 * n_heads, tq, 1), F32),
            pltpu.VMEM((2 * n_heads, tq, hw), F32),
        ],
        compiler_params=_params("parallel", "arbitrary"),
        name="attn_sample",
    )(q, cache_k, cache_v, k_new, v_new, *lam_vecs, g_subln)


def _pool_mix_kernel(u_ref, prev_ref, hist_ref, wp_ref, sc_ref, o_ref, ext_ref, *, tt, pos0):
    t = pl.program_id(1)
    ext_ref[0:HIST_ROWS, :] = jnp.where(t == 0, hist_ref[...], prev_ref[...])
    ext_ref[HIST_ROWS:HIST_ROWS + tt, :] = u_ref[...]
    pos = pos0 + t * tt + lax.broadcasted_iota(jnp.int32, (tt, 1), 0)
    gw = u_ref.shape[1] // len(POOL_WINDOWS)
    for g, w in enumerate(POOL_WINDOWS):
        cols = slice(g * gw, (g + 1) * gw)
        win = ext_ref[HIST_ROWS:HIST_ROWS + tt, cols]
        for back in range(1, w):
            win = win + ext_ref[HIST_ROWS - back:HIST_ROWS - back + tt, cols]
        count = jnp.minimum(pos + 1, w).astype(F32)
        pooled = win / count - u_ref[:, cols]
        mixed = _dot(pooled.astype(BF16), wp_ref[g]) * sc_ref[:, cols]
        o_ref[:, cols] = mixed.astype(o_ref.dtype)


def _pool_mix(u, hist, w_pool, pool_scale, pos0):
    b, t, width = u.shape
    tt = _tile(t, 256)
    assert tt % HIST_ROWS == 0
    hist16 = jnp.pad(hist, ((0, 0), (HIST_ROWS - POOL_HIST, 0), (0, 0)))
    per = tt // HIST_ROWS
    return pl.pallas_call(
        functools.partial(_pool_mix_kernel, tt=tt, pos0=pos0),
        grid=(b, t // tt),
        in_specs=[
            pl.BlockSpec((None, tt, width), lambda bi, ti: (bi, ti, 0)),
            pl.BlockSpec((None, HIST_ROWS, width), lambda bi, ti: (bi, jnp.maximum(ti * per - 1, 0), 0)),
            pl.BlockSpec((None, HIST_ROWS, width), lambda bi, ti: (bi, 0, 0)),
            pl.BlockSpec(w_pool.shape, lambda bi, ti: (0, 0, 0)),
            pl.BlockSpec((1, width), lambda bi, ti: (0, 0)),
        ],
        out_specs=pl.BlockSpec((None, tt, width), lambda bi, ti: (bi, ti, 0)),
        out_shape=jax.ShapeDtypeStruct((b, t, width), BF16),
        scratch_shapes=[pltpu.VMEM((HIST_ROWS + tt, width), F32)],
        compiler_params=_params("parallel", "parallel"),
        name="pool_mix",
    )(u, u, hist16, w_pool, pool_scale)


def _trunk_layer(x, pos0, pool_hist, kv_cache, lam_init, w):
    b, t, dm = x.shape
    m = b * t
    n_heads, d = w["n_heads"], w["head_dim"]
    qk_w = n_heads * 2 * d
    attn_w, pool_w = qk_w, w["w_pool_out"].shape[0]
    c_k, c_v, c_u, c_g = qk_w, 2 * qk_w, 2 * qk_w + attn_w, 2 * qk_w + attn_w + pool_w
    x2 = x.reshape(m, dm)
    w_in = w["w_in"]

    xn = _rmsnorm_cast(x2, w["g_norm1"], NORM_EPS)
    (q,) = _tiled_call(functools.partial(_q_body, scale=d ** -0.5), [xn], [(w_in, 0)], [], [w["g_q"]],
                       [BF16], qk_w, 1024, 1024, "proj_q")
    k_f, k_b = _tiled_call(_k_body, [xn], [(w_in, c_k)], [], [w["g_k"]], [F32, BF16], qk_w, 1024, 1024,
                           "proj_k")
    v_f, v_b = _tiled_call(_v_body, [xn], [(w_in, c_v)], [], [], [F32, BF16], attn_w, 1024, 1024, "proj_v")
    (u,) = _tiled_call(_u_body, [xn], [(w_in, c_u)], [], [], [F32], pool_w, 1024, 1024, "proj_u")
    (gates,) = _tiled_call(_gate_body, [xn], [(w_in, c_g)], [], [], [BF16], 2 * dm, 1024, 1024, "proj_gates")

    lam_vecs = w["lam_vecs"]
    if kv_cache is None:
        o = _attn_prompt(q.reshape(b, t, qk_w), k_b.reshape(b, t, qk_w), v_b.reshape(b, t, attn_w),
                         lam_vecs, w["g_subln"], n_heads, lam_init)
    else:
        ck, cv = kv_cache
        o = _attn_sample(q.reshape(b, t, qk_w), ck, cv, k_b.reshape(b, t, qk_w), v_b.reshape(b, t, attn_w),
                         lam_vecs, w["g_subln"], n_heads, lam_init)

    u3 = u.reshape(b, t, pool_w)
    yb_in = _pool_mix(u3, pool_hist, w["w_pool"], w["pool_scale"], pos0)
    assert t >= POOL_HIST
    pool_state = u3[:, t - POOL_HIST:, :]

    (merged,) = _tiled_call(_merge_body, [o.reshape(m, attn_w), yb_in.reshape(m, pool_w)],
                            [(w["w_attn_out"], 0), (w["w_pool_out"], 0)], [(gates, 0), (gates, dm)], [],
                            [BF16], dm, 1024, 1024, "merge")
    (x1,) = _tiled_call(_residual_body, [merged], [(w["w_o"], 0)], [(x2, 0)], [], [F32], dm, 1024, 1024,
                        "out_proj")
    hn = _rmsnorm_cast(x1, w["g_norm2"], NORM_EPS)
    (act,) = _tiled_call(_relu2_body, [hn], [(w["w_up"], 0)], [], [], [BF16], w["w_up"].shape[1], 1024, 1024,
                         "mlp_up")
    y = _down_proj(act, w["w_down"], x1, 1024, 1024, 2048)
    return (y.reshape(b, t, dm), k_f.reshape(b, t, n_heads, 2, d), v_f.reshape(b, t, n_heads, 2 * d),
            pool_state)


def kernel(x_prompt, x_sample, cache_k, cache_v, state_pool, g_norm1, w_in, g_q, g_k, lambda_q1, lambda_k1,
           lambda_q2, lambda_k2, g_subln, w_attn_out, w_pool, pool_scale, w_pool_out, w_o, g_norm2, w_up,
           w_down):
    depth, dec_b, past, n_heads, _, d = cache_k.shape
    y_p, y_s = x_prompt, x_sample
    hist_p = jnp.zeros((x_prompt.shape[0], POOL_HIST, w_pool_out.shape[1]), F32)
    outs = [[] for _ in range(6)]
    for l in range(depth):
        lam_init = 0.8 - 0.6 * math.exp(-0.3 * l)
        row = lambda a: a[l].reshape(1, -1).astype(F32)
        w = dict(
            n_heads=n_heads, head_dim=d,
            g_norm1=g_norm1[l], g_norm2=g_norm2[l], g_q=row(g_q), g_k=row(g_k), g_subln=row(g_subln),
            lam_vecs=(row(lambda_q1), row(lambda_k1), row(lambda_q2), row(lambda_k2)),
            pool_scale=row(pool_scale),
            w_in=w_in[l].astype(BF16), w_attn_out=w_attn_out[l].astype(BF16), w_pool=w_pool[l].astype(BF16),
            w_pool_out=w_pool_out[l].astype(BF16), w_o=w_o[l].astype(BF16), w_up=w_up[l].astype(BF16),
            w_down=w_down[l].astype(BF16),
        )
        y_p, kp, vp, sp = _trunk_layer(y_p, 0, hist_p, None, lam_init, w)
        kv = (cache_k[l].reshape(dec_b, past, -1), cache_v[l].reshape(dec_b, past, -1))
        y_s, ks, vs, ss = _trunk_layer(y_s, past, state_pool[l], kv, lam_init, w)
        for lst, val in zip(outs, (kp, vp, sp, ks, vs, ss)):
            lst.append(val)
    return (y_p, y_s) + tuple(jnp.stack(o) for o in outs)
```

```python
import functools
import math

import jax
import jax.numpy as jnp
from jax import lax
from jax.experimental import pallas as pl
from jax.experimental.pallas import tpu as pltpu

CHUNK = 64
POOL_WINDOWS = (2, 4, 8, 16)
POOL_HIST = max(POOL_WINDOWS) - 1
HIST_ROWS = 16
NORM_EPS = 1e-6
SUBLN_EPS = 1e-5
NEG_INF = -1e30
V7X_VMEM_LIMIT_BYTES = 56 * 1024 * 1024
STAT_LANES = 128

F32 = jnp.float32
BF16 = jnp.bfloat16


def _tile(dim, pref):
    t = min(dim, pref)
    while dim % t:
        t -= 1
    return t


def _params(*sem):
    return pltpu.CompilerParams(dimension_semantics=sem, vmem_limit_bytes=V7X_VMEM_LIMIT_BYTES)


def _dot(a, b):
    return jnp.dot(a, b, preferred_element_type=F32)


def _dot_nt(a, b):
    return lax.dot_general(a, b, (((1,), (1,)), ((), ())), preferred_element_type=F32)


def _rmsnorm_cast_kernel(x_ref, g_ref, o_ref, *, eps):
    x = x_ref[...]
    ms = jnp.mean(x * x, axis=-1, keepdims=True)
    o_ref[...] = (x * lax.rsqrt(ms + eps) * g_ref[...]).astype(o_ref.dtype)


def _rmsnorm_cast(x, g, eps):
    m, d = x.shape
    tm = _tile(m, 256)
    return pl.pallas_call(
        functools.partial(_rmsnorm_cast_kernel, eps=eps),
        grid=(m // tm,),
        in_specs=[pl.BlockSpec((tm, d), lambda i: (i, 0)), pl.BlockSpec((1, d), lambda i: (0, 0))],
        out_specs=pl.BlockSpec((tm, d), lambda i: (i, 0)),
        out_shape=jax.ShapeDtypeStruct((m, d), BF16),
        compiler_params=_params("parallel"),
        name="rmsnorm_cast",
    )(x, g.reshape(1, d).astype(F32))


def _tiled_call(body, rows, weights, tiles, vecs, out_dtypes, n_cols, tm, tn, name):
    m = rows[0].shape[0]
    tm = _tile(m, tm)
    tn = _tile(n_cols, tn)
    in_specs, args = [], []
    for a in rows:
        in_specs.append(pl.BlockSpec((tm, a.shape[1]), lambda i, j: (i, 0)))
        args.append(a)
    for w, c0 in weights:
        assert c0 % tn == 0
        in_specs.append(pl.BlockSpec((w.shape[0], tn), lambda i, j, o=c0 // tn: (0, j + o)))
        args.append(w)
    for t, c0 in tiles:
        assert c0 % tn == 0
        in_specs.append(pl.BlockSpec((tm, tn), lambda i, j, o=c0 // tn: (i, j + o)))
        args.append(t)
    for v in vecs:
        in_specs.append(pl.BlockSpec(v.shape, lambda i, j: (0, 0)))
        args.append(v)
    out_specs = [pl.BlockSpec((tm, tn), lambda i, j: (i, j)) for _ in out_dtypes]
    out_shape = [jax.ShapeDtypeStruct((m, n_cols), dt) for dt in out_dtypes]
    return pl.pallas_call(
        body,
        grid=(m // tm, n_cols // tn),
        in_specs=in_specs,
        out_specs=out_specs,
        out_shape=out_shape,
        compiler_params=_params("parallel", "parallel"),
        name=name,
    )(*args)


def _head_rmsnorm_store(acc, g, eps, scale, out_refs):
    group = g.shape[1]
    for c in range(acc.shape[1] // group):
        sl = slice(c * group, (c + 1) * group)
        blk = acc[:, sl]
        ms = jnp.mean(blk * blk, axis=-1, keepdims=True)
        y = blk * lax.rsqrt(ms + eps) * g
        for ref, s in zip(out_refs, scale):
            ref[:, sl] = (y if s == 1.0 else y * s).astype(ref.dtype)


def _q_body(a_ref, w_ref, g_ref, q_ref, *, scale):
    _head_rmsnorm_store(_dot(a_ref[...], w_ref[...]), g_ref[...], NORM_EPS, (scale,), (q_ref,))


def _k_body(a_ref, w_ref, g_ref, kf_ref, kb_ref):
    _head_rmsnorm_store(_dot(a_ref[...], w_ref[...]), g_ref[...], NORM_EPS, (1.0, 1.0), (kf_ref, kb_ref))


def _v_body(a_ref, w_ref, vf_ref, vb_ref):
    acc = _dot(a_ref[...], w_ref[...])
    vf_ref[...] = acc
    vb_ref[...] = acc.astype(vb_ref.dtype)


def _u_body(a_ref, w_ref, u_ref):
    u_ref[...] = _dot(a_ref[...], w_ref[...])


def _gate_body(a_ref, w_ref, g_ref):
    g_ref[...] = jax.nn.sigmoid(_dot(a_ref[...], w_ref[...])).astype(g_ref.dtype)


def _merge_body(o_ref, yb_ref, wa_ref, wp_ref, ga_ref, gb_ref, out_ref):
    ya = _dot(o_ref[...], wa_ref[...])
    yb = _dot(yb_ref[...], wp_ref[...])
    out_ref[...] = (ga_ref[...].astype(F32) * ya + gb_ref[...].astype(F32) * yb).astype(out_ref.dtype)


def _residual_body(a_ref, w_ref, x_ref, out_ref):
    out_ref[...] = x_ref[...] + _dot(a_ref[...], w_ref[...])


def _relu2_body(a_ref, w_ref, out_ref):
    h = jnp.maximum(_dot(a_ref[...], w_ref[...]), 0.0)
    out_ref[...] = (h * h).astype(out_ref.dtype)


def _down_body(a_ref, w_ref, x_ref, out_ref):
    @pl.when(pl.program_id(2) == 0)
    def _():
        out_ref[...] = x_ref[...]

    out_ref[...] += _dot(a_ref[...], w_ref[...])


def _down_proj(a, w, x, tm, tn, tk):
    m, kdim = a.shape
    n = w.shape[1]
    tm, tn, tk = _tile(m, tm), _tile(n, tn), _tile(kdim, tk)
    return pl.pallas_call(
        _down_body,
        grid=(m // tm, n // tn, kdim // tk),
        in_specs=[
            pl.BlockSpec((tm, tk), lambda i, j, k: (i, k)),
            pl.BlockSpec((tk, tn), lambda i, j, k: (k, j)),
            pl.BlockSpec((tm, tn), lambda i, j, k: (i, j)),
        ],
        out_specs=pl.BlockSpec((tm, tn), lambda i, j, k: (i, j)),
        out_shape=jax.ShapeDtypeStruct((m, n), F32),
        compiler_params=_params("parallel", "parallel", "arbitrary"),
        name="down_proj",
    )(a, w, x)


def _rep(x, width):
    if width <= STAT_LANES:
        return x[:, :width]
    return jnp.concatenate([x] * (width // STAT_LANES), axis=1)


def _lambda(lq1, lk1, lq2, lk2, lam_init):
    return (jnp.exp(jnp.sum(lq1[...] * lk1[...], keepdims=True))
            - jnp.exp(jnp.sum(lq2[...] * lk2[...], keepdims=True)) + lam_init)


def _subln(o, gs, lam_init):
    ms = jnp.mean(o * o, axis=-1, keepdims=True)
    return o * lax.rsqrt(ms + SUBLN_EPS) * gs * (1.0 - lam_init)


def _attn_prompt_kernel(q_ref, k_ref, v_ref, lq1, lk1, lq2, lk2, gs_ref, o_ref, *, tq, d, lam_init):
    t = q_ref.shape[0]
    lam = _lambda(lq1, lk1, lq2, lk2, lam_init)
    gs = gs_ref[...]
    rq = lax.broadcasted_iota(jnp.int32, (tq, tq), 0) // CHUNK
    rk = lax.broadcasted_iota(jnp.int32, (tq, tq), 1) // CHUNK
    diag_mask = rk <= rq
    for i in range(t // tq):
        r0 = i * tq
        w_diag, w_prev = None, None
        for c in range(2):
            qc = q_ref[r0:r0 + tq, c * d:(c + 1) * d]
            s_diag = jnp.where(diag_mask, _dot_nt(qc, k_ref[r0:r0 + tq, c * d:(c + 1) * d]), NEG_INF)
            m = jnp.max(s_diag, axis=-1, keepdims=True)
            if i > 0:
                s_prev = _dot_nt(qc, k_ref[0:r0, c * d:(c + 1) * d])
                m = jnp.maximum(m, jnp.max(s_prev, axis=-1, keepdims=True))
            p_diag = jnp.exp(s_diag - m)
            l = jnp.sum(p_diag, axis=-1, keepdims=True)
            if i > 0:
                p_prev = jnp.exp(s_prev - m)
                l = l + jnp.sum(p_prev, axis=-1, keepdims=True)
            coef = 1.0 / l if c == 0 else -lam / l
            w_diag = p_diag * coef if c == 0 else w_diag + p_diag * coef
            if i > 0:
                w_prev = p_prev * coef if c == 0 else w_prev + p_prev * coef
        o = _dot(w_diag.astype(BF16), v_ref[r0:r0 + tq, :])
        if i > 0:
            o = o + _dot(w_prev.astype(BF16), v_ref[0:r0, :])
        o_ref[r0:r0 + tq, :] = _subln(o, gs, lam_init).astype(o_ref.dtype)


def _attn_prompt(q, k, v, lam_vecs, g_subln, n_heads, lam_init):
    b, t, width = q.shape
    hw = width // n_heads
    d = hw // 2
    tq = _tile(t, 256)
    assert tq % CHUNK == 0
    vec = pl.BlockSpec((1, d), lambda bi, h: (0, 0))
    seq = pl.BlockSpec((None, t, hw), lambda bi, h: (bi, 0, h))
    return pl.pallas_call(
        functools.partial(_attn_prompt_kernel, tq=tq, d=d, lam_init=lam_init),
        grid=(b, n_heads),
        in_specs=[seq, seq, seq, vec, vec, vec, vec, pl.BlockSpec((1, hw), lambda bi, h: (0, 0))],
        out_specs=seq,
        out_shape=jax.ShapeDtypeStruct((b, t, width), BF16),
        compiler_params=_params("parallel", "parallel"),
        name="attn_prompt",
    )(q, k, v, *lam_vecs, g_subln)


def _online_softmax_step(idx, s, v, m_sc, l_sc, acc_sc):
    m_old = m_sc[idx]
    m_new = jnp.maximum(m_old, jnp.max(s, axis=-1, keepdims=True))
    alpha = jnp.exp(m_old - m_new)
    p = jnp.exp(s - _rep(m_new, s.shape[1]))
    l_sc[idx] = alpha * l_sc[idx] + jnp.sum(p, axis=-1, keepdims=True)
    acc_sc[idx] = _rep(alpha, v.shape[1]) * acc_sc[idx] + _dot(p.astype(v.dtype), v)
    m_sc[idx] = m_new


def _attn_sample_kernel(q_ref, ck_ref, cv0_ref, cv1_ref, kn_ref, vn_ref, lq1, lk1, lq2, lk2, gs_ref, o_ref,
                        m_sc, l_sc, acc_sc, *, n_heads, d, tk, past, lam_init):
    kb = pl.program_id(1)
    nkb = pl.num_programs(1) - 1
    hw = 2 * d

    @pl.when(kb == 0)
    def _():
        m_sc[...] = jnp.full(m_sc.shape, -jnp.inf, F32)
        l_sc[...] = jnp.zeros(l_sc.shape, F32)
        acc_sc[...] = jnp.zeros(acc_sc.shape, F32)

    def update(k_of, v_of, mask):
        for h in range(n_heads):
            v = v_of(h)
            for c in range(2):
                s = _dot_nt(q_ref[:, h * hw + c * d:h * hw + (c + 1) * d], k_of(h, c))
                if mask is not None:
                    s = jnp.where(mask, s, NEG_INF)
                _online_softmax_step(2 * h + c, s, v, m_sc, l_sc, acc_sc)

    @pl.when(kb < nkb)
    def _():
        update(lambda h, c: ck_ref[pl.ds(2 * h + c, tk, stride=2 * n_heads), :].astype(BF16),
               lambda h: jnp.concatenate([cv0_ref[pl.ds(h, tk, stride=n_heads), :],
                                          cv1_ref[pl.ds(h, tk, stride=n_heads), :]], axis=1).astype(BF16),
               None)

    @pl.when(kb == nkb)
    def _():
        tq = q_ref.shape[0]
        rq = (past + lax.broadcasted_iota(jnp.int32, (tq, tq), 0)) // CHUNK
        rk = (past + lax.broadcasted_iota(jnp.int32, (tq, tq), 1)) // CHUNK
        update(lambda h, c: kn_ref[:, h * hw + c * d:h * hw + (c + 1) * d],
               lambda h: vn_ref[:, h * hw:(h + 1) * hw], rk <= rq)
        lam = _lambda(lq1, lk1, lq2, lk2, lam_init)
        for h in range(n_heads):
            o = (acc_sc[2 * h] / _rep(l_sc[2 * h], hw)
                 - lam * (acc_sc[2 * h + 1] / _rep(l_sc[2 * h + 1], hw)))
            o_ref[:, h * hw:(h + 1) * hw] = _subln(o, gs_ref[...], lam_init).astype(o_ref.dtype)


def _attn_sample(q, cache_k, cache_v, k_new, v_new, lam_vecs, g_subln, n_heads, lam_init):
    b, tq, width = q.shape
    hw = width // n_heads
    d = hw // 2
    assert d == STAT_LANES
    past = cache_v.shape[1] // n_heads
    tk = _tile(past, 512)
    nkb = past // tk
    vec = pl.BlockSpec((1, d), lambda bi, kb: (0, 0))
    new = pl.BlockSpec((None, tq, width), lambda bi, kb: (bi, 0, 0))
    ck_spec = pl.BlockSpec((None, tk * 2 * n_heads, d), lambda bi, kb: (bi, jnp.minimum(kb, nkb - 1), 0))
    cv_specs = [pl.BlockSpec((None, tk * n_heads, d), lambda bi, kb, c=c: (bi, jnp.minimum(kb, nkb - 1), c))
                for c in range(2)]
    return pl.pallas_call(
        functools.partial(_attn_sample_kernel, n_heads=n_heads, d=d, tk=tk, past=past, lam_init=lam_init),
        grid=(b, nkb + 1),
        in_specs=[new, ck_spec, *cv_specs, new, new, vec, vec, vec, vec,
                  pl.BlockSpec((1, hw), lambda bi, kb: (0, 0))],
        out_specs=new,
        out_shape=jax.ShapeDtypeStruct((b, tq, width), BF16),
        scratch_shapes=[
            pltpu.VMEM((2 * n_heads, tq, STAT_LANES), F32),
            pltpu.VMEM((2 * n_heads, tq, STAT_LANES), F32),
            pltpu.VMEM((2 * n_heads, tq, hw), F32),
        ],
        compiler_params=_params("parallel", "arbitrary"),
        name="attn_sample",
    )(q, cache_k, cache_v, cache_v, k_new, v_new, *lam_vecs, g_subln)


def _pool_mix_kernel(u_ref, prev_ref, hist_ref, wp_ref, sc_ref, o_ref, ext_ref, *, tt, pos0):
    t = pl.program_id(1)
    ext_ref[0:HIST_ROWS, :] = jnp.where(t == 0, hist_ref[...], prev_ref[...])
    ext_ref[HIST_ROWS:HIST_ROWS + tt, :] = u_ref[...]
    pos = pos0 + t * tt + lax.broadcasted_iota(jnp.int32, (tt, 1), 0)
    gw = u_ref.shape[1] // len(POOL_WINDOWS)
    for g, w in enumerate(POOL_WINDOWS):
        cols = slice(g * gw, (g + 1) * gw)
        win = ext_ref[HIST_ROWS:HIST_ROWS + tt, cols]
        for back in range(1, w):
            win = win + ext_ref[HIST_ROWS - back:HIST_ROWS - back + tt, cols]
        count = jnp.minimum(pos + 1, w).astype(F32)
        pooled = win / count - u_ref[:, cols]
        mixed = _dot(pooled.astype(BF16), wp_ref[g]) * sc_ref[:, cols]
        o_ref[:, cols] = mixed.astype(o_ref.dtype)


def _pool_mix(u, hist, w_pool, pool_scale, pos0):
    b, t, width = u.shape
    tt = _tile(t, 256)
    assert tt % HIST_ROWS == 0
    hist16 = jnp.pad(hist, ((0, 0), (HIST_ROWS - POOL_HIST, 0), (0, 0)))
    per = tt // HIST_ROWS
    return pl.pallas_call(
        functools.partial(_pool_mix_kernel, tt=tt, pos0=pos0),
        grid=(b, t // tt),
        in_specs=[
            pl.BlockSpec((None, tt, width), lambda bi, ti: (bi, ti, 0)),
            pl.BlockSpec((None, HIST_ROWS, width), lambda bi, ti: (bi, jnp.maximum(ti * per - 1, 0), 0)),
            pl.BlockSpec((None, HIST_ROWS, width), lambda bi, ti: (bi, 0, 0)),
            pl.BlockSpec(w_pool.shape, lambda bi, ti: (0, 0, 0)),
            pl.BlockSpec((1, width), lambda bi, ti: (0, 0)),
        ],
        out_specs=pl.BlockSpec((None, tt, width), lambda bi, ti: (bi, ti, 0)),
        out_shape=jax.ShapeDtypeStruct((b, t, width), BF16),
        scratch_shapes=[pltpu.VMEM((HIST_ROWS + tt, width), F32)],
        compiler_params=_params("parallel", "parallel"),
        name="pool_mix",
    )(u, u, hist16, w_pool, pool_scale)


def _trunk_layer(x, pos0, pool_hist, kv_cache, lam_init, w):
    b, t, dm = x.shape
    m = b * t
    n_heads, d = w["n_heads"], w["head_dim"]
    qk_w = n_heads * 2 * d
    attn_w, pool_w = qk_w, w["w_pool_out"].shape[0]
    c_k, c_v, c_u, c_g = qk_w, 2 * qk_w, 2 * qk_w + attn_w, 2 * qk_w + attn_w + pool_w
    x2 = x.reshape(m, dm)
    w_in = w["w_in"]

    xn = _rmsnorm_cast(x2, w["g_norm1"], NORM_EPS)
    (q,) = _tiled_call(functools.partial(_q_body, scale=d ** -0.5), [xn], [(w_in, 0)], [], [w["g_q"]],
                       [BF16], qk_w, 1024, 1024, "proj_q")
    k_f, k_b = _tiled_call(_k_body, [xn], [(w_in, c_k)], [], [w["g_k"]], [F32, BF16], qk_w, 1024, 1024,
                           "proj_k")
    v_f, v_b = _tiled_call(_v_body, [xn], [(w_in, c_v)], [], [], [F32, BF16], attn_w, 1024, 1024, "proj_v")
    (u,) = _tiled_call(_u_body, [xn], [(w_in, c_u)], [], [], [F32], pool_w, 1024, 1024, "proj_u")
    (gates,) = _tiled_call(_gate_body, [xn], [(w_in, c_g)], [], [], [BF16], 2 * dm, 1024, 1024, "proj_gates")

    lam_vecs = w["lam_vecs"]
    if kv_cache is None:
        o = _attn_prompt(q.reshape(b, t, qk_w), k_b.reshape(b, t, qk_w), v_b.reshape(b, t, attn_w),
                         lam_vecs, w["g_subln"], n_heads, lam_init)
    else:
        ck, cv = kv_cache
        o = _attn_sample(q.reshape(b, t, qk_w), ck, cv, k_b.reshape(b, t, qk_w), v_b.reshape(b, t, attn_w),
                         lam_vecs, w["g_subln"], n_heads, lam_init)

    u3 = u.reshape(b, t, pool_w)
    yb_in = _pool_mix(u3, pool_hist, w["w_pool"], w["pool_scale"], pos0)
    assert t >= POOL_HIST
    pool_state = u3[:, t - POOL_HIST:, :]

    (merged,) = _tiled_call(_merge_body, [o.reshape(m, attn_w), yb_in.reshape(m, pool_w)],
                            [(w["w_attn_out"], 0), (w["w_pool_out"], 0)], [(gates, 0), (gates, dm)], [],
                            [BF16], dm, 1024, 1024, "merge")
    (x1,) = _tiled_call(_residual_body, [merged], [(w["w_o"], 0)], [(x2, 0)], [], [F32], dm, 1024, 1024,
                        "out_proj")
    hn = _rmsnorm_cast(x1, w["g_norm2"], NORM_EPS)
    (act,) = _tiled_call(_relu2_body, [hn], [(w["w_up"], 0)], [], [], [BF16], w["w_up"].shape[1], 1024, 1024,
                         "mlp_up")
    y = _down_proj(act, w["w_down"], x1, 1024, 1024, 2048)
    return (y.reshape(b, t, dm), k_f.reshape(b, t, n_heads, 2, d), v_f.reshape(b, t, n_heads, 2 * d),
            pool_state)


def kernel(x_prompt, x_sample, cache_k, cache_v, state_pool, g_norm1, w_in, g_q, g_k, lambda_q1, lambda_k1,
           lambda_q2, lambda_k2, g_subln, w_attn_out, w_pool, pool_scale, w_pool_out, w_o, g_norm2, w_up,
           w_down):
    depth, dec_b, past, n_heads, _, d = cache_k.shape
    y_p, y_s = x_prompt, x_sample
    hist_p = jnp.zeros((x_prompt.shape[0], POOL_HIST, w_pool_out.shape[1]), F32)
    outs = [[] for _ in range(6)]
    for l in range(depth):
        lam_init = 0.8 - 0.6 * math.exp(-0.3 * l)
        row = lambda a: a[l].reshape(1, -1).astype(F32)
        w = dict(
            n_heads=n_heads, head_dim=d,
            g_norm1=g_norm1[l], g_norm2=g_norm2[l], g_q=row(g_q), g_k=row(g_k), g_subln=row(g_subln),
            lam_vecs=(row(lambda_q1), row(lambda_k1), row(lambda_q2), row(lambda_k2)),
            pool_scale=row(pool_scale),
            w_in=w_in[l].astype(BF16), w_attn_out=w_attn_out[l].astype(BF16), w_pool=w_pool[l].astype(BF16),
            w_pool_out=w_pool_out[l].astype(BF16), w_o=w_o[l].astype(BF16), w_up=w_up[l].astype(BF16),
            w_down=w_down[l].astype(BF16),
        )
        y_p, kp, vp, sp = _trunk_layer(y_p, 0, hist_p, None, lam_init, w)
        kv = (cache_k[l].reshape(dec_b, past * n_heads * 2, d), cache_v[l].reshape(dec_b, past * n_heads, 2 * d))
        y_s, ks, vs, ss = _trunk_layer(y_s, past, state_pool[l], kv, lam_init, w)
        for lst, val in zip(outs, (kp, vp, sp, ks, vs, ss)):
            lst.append(val)
    return (y_p, y_s) + tuple(jnp.stack(o) for o in outs)
```

```python
import functools
import math

import jax
import jax.numpy as jnp
from jax import lax
from jax.experimental import pallas as pl
from jax.experimental.pallas import tpu as pltpu

CHUNK = 64
POOL_WINDOWS = (2, 4, 8, 16)
POOL_HIST = max(POOL_WINDOWS) - 1
HIST_ROWS = 16
NORM_EPS = 1e-6
SUBLN_EPS = 1e-5
NEG_INF = -1e30
V7X_VMEM_LIMIT_BYTES = 56 * 1024 * 1024
STAT_LANES = 128
BF16_SUBLANES = 16

F32 = jnp.float32
BF16 = jnp.bfloat16


def _tile(dim, pref):
    t = min(dim, pref)
    while dim % t:
        t -= 1
    return t


def _params(*sem):
    return pltpu.CompilerParams(dimension_semantics=sem, vmem_limit_bytes=V7X_VMEM_LIMIT_BYTES)


def _dot(a, b):
    return jnp.dot(a, b, preferred_element_type=F32)


def _dot_nt(a, b):
    return lax.dot_general(a, b, (((1,), (1,)), ((), ())), preferred_element_type=F32)


def _rmsnorm_cast_kernel(x_ref, g_ref, o_ref, *, eps):
    x = x_ref[...]
    ms = jnp.mean(x * x, axis=-1, keepdims=True)
    o_ref[...] = (x * lax.rsqrt(ms + eps) * g_ref[...]).astype(o_ref.dtype)


def _rmsnorm_cast(x, g, eps):
    m, d = x.shape
    tm = _tile(m, 256)
    return pl.pallas_call(
        functools.partial(_rmsnorm_cast_kernel, eps=eps),
        grid=(m // tm,),
        in_specs=[pl.BlockSpec((tm, d), lambda i: (i, 0)), pl.BlockSpec((1, d), lambda i: (0, 0))],
        out_specs=pl.BlockSpec((tm, d), lambda i: (i, 0)),
        out_shape=jax.ShapeDtypeStruct((m, d), BF16),
        compiler_params=_params("parallel"),
        name="rmsnorm_cast",
    )(x, g.reshape(1, d).astype(F32))


def _with_casts(body, n_in, n_out, n_cast):
    def wrapped(*refs):
        main_in, cast_in = refs[:n_in], refs[n_in:n_in + n_cast]
        outs = refs[n_in + n_cast:]
        body(*main_in, *outs[:n_out])
        for src, dst in zip(cast_in, outs[n_out:]):
            dst[...] = src[...].astype(dst.dtype)
    return wrapped


def _tiled_call(body, rows, weights, tiles, vecs, out_dtypes, n_cols, tm, tn, name, casts=()):
    m = rows[0].shape[0]
    tm = _tile(m, tm)
    tn = _tile(n_cols, tn)
    steps = (m // tm) * (n_cols // tn)
    nj = n_cols // tn
    hosted = [c for c in casts if c[0].shape[0] % (steps * BF16_SUBLANES) == 0 and c[1] % c[2] == 0]
    if len(hosted) != len(casts):
        res = _tiled_call(body, rows, weights, tiles, vecs, out_dtypes, n_cols, tm, tn, name)
        return list(res) + [c[0][:, c[1]:c[1] + c[2]].astype(BF16) for c in casts]
    in_specs, args = [], []
    for a in rows:
        in_specs.append(pl.BlockSpec((tm, a.shape[1]), lambda i, j: (i, 0)))
        args.append(a)
    for w, c0 in weights:
        assert c0 % tn == 0
        in_specs.append(pl.BlockSpec((w.shape[0], tn), lambda i, j, o=c0 // tn: (0, j + o)))
        args.append(w)
    for t, c0 in tiles:
        assert c0 % tn == 0
        in_specs.append(pl.BlockSpec((tm, tn), lambda i, j, o=c0 // tn: (i, j + o)))
        args.append(t)
    for v in vecs:
        in_specs.append(pl.BlockSpec(v.shape, lambda i, j: (0, 0)))
        args.append(v)
    n_in = len(args)
    out_specs = [pl.BlockSpec((tm, tn), lambda i, j: (i, j)) for _ in out_dtypes]
    out_shape = [jax.ShapeDtypeStruct((m, n_cols), dt) for dt in out_dtypes]
    for src, c0, n in casts:
        slab = src.shape[0] // steps
        in_specs.append(pl.BlockSpec((slab, n), lambda i, j, o=c0 // n: (i * nj + j, o)))
        args.append(src)
        out_specs.append(pl.BlockSpec((slab, n), lambda i, j: (i * nj + j, 0)))
        out_shape.append(jax.ShapeDtypeStruct((src.shape[0], n), BF16))
    if casts:
        body = _with_casts(body, n_in, len(out_dtypes), len(casts))
    return pl.pallas_call(
        body,
        grid=(m // tm, n_cols // tn),
        in_specs=in_specs,
        out_specs=out_specs,
        out_shape=out_shape,
        compiler_params=_params("parallel", "parallel"),
        name=name,
    )(*args)


def _head_rmsnorm_store(acc, g, eps, scale, out_refs):
    group = g.shape[1]
    for c in range(acc.shape[1] // group):
        sl = slice(c * group, (c + 1) * group)
        blk = acc[:, sl]
        ms = jnp.mean(blk * blk, axis=-1, keepdims=True)
        y = blk * lax.rsqrt(ms + eps) * g
        for ref, s in zip(out_refs, scale):
            ref[:, sl] = (y if s == 1.0 else y * s).astype(ref.dtype)


def _q_body(a_ref, w_ref, g_ref, q_ref, *, scale):
    _head_rmsnorm_store(_dot(a_ref[...], w_ref[...]), g_ref[...], NORM_EPS, (scale,), (q_ref,))


def _k_body(a_ref, w_ref, g_ref, kf_ref, kb_ref):
    _head_rmsnorm_store(_dot(a_ref[...], w_ref[...]), g_ref[...], NORM_EPS, (1.0, 1.0), (kf_ref, kb_ref))


def _v_body(a_ref, w_ref, vf_ref, vb_ref):
    acc = _dot(a_ref[...], w_ref[...])
    vf_ref[...] = acc
    vb_ref[...] = acc.astype(vb_ref.dtype)


def _u_body(a_ref, w_ref, u_ref):
    u_ref[...] = _dot(a_ref[...], w_ref[...])


def _gate_body(a_ref, w_ref, g_ref):
    g_ref[...] = jax.nn.sigmoid(_dot(a_ref[...], w_ref[...])).astype(g_ref.dtype)


def _merge_body(o_ref, yb_ref, wa_ref, wp_ref, ga_ref, gb_ref, out_ref):
    ya = _dot(o_ref[...], wa_ref[...])
    yb = _dot(yb_ref[...], wp_ref[...])
    out_ref[...] = (ga_ref[...].astype(F32) * ya + gb_ref[...].astype(F32) * yb).astype(out_ref.dtype)


def _residual_body(a_ref, w_ref, x_ref, out_ref):
    out_ref[...] = x_ref[...] + _dot(a_ref[...], w_ref[...])


def _relu2_body(a_ref, w_ref, out_ref):
    h = jnp.maximum(_dot(a_ref[...], w_ref[...]), 0.0)
    out_ref[...] = (h * h).astype(out_ref.dtype)


def _down_body(a_ref, w_ref, x_ref, out_ref):
    @pl.when(pl.program_id(2) == 0)
    def _():
        out_ref[...] = x_ref[...]

    out_ref[...] += _dot(a_ref[...], w_ref[...])


def _down_proj(a, w, x, tm, tn, tk):
    m, kdim = a.shape
    n = w.shape[1]
    tm, tn, tk = _tile(m, tm), _tile(n, tn), _tile(kdim, tk)
    return pl.pallas_call(
        _down_body,
        grid=(m // tm, n // tn, kdim // tk),
        in_specs=[
            pl.BlockSpec((tm, tk), lambda i, j, k: (i, k)),
            pl.BlockSpec((tk, tn), lambda i, j, k: (k, j)),
            pl.BlockSpec((tm, tn), lambda i, j, k: (i, j)),
        ],
        out_specs=pl.BlockSpec((tm, tn), lambda i, j, k: (i, j)),
        out_shape=jax.ShapeDtypeStruct((m, n), F32),
        compiler_params=_params("parallel", "parallel", "arbitrary"),
        name="down_proj",
    )(a, w, x)


def _rep(x, width):
    if width <= STAT_LANES:
        return x[:, :width]
    return jnp.concatenate([x] * (width // STAT_LANES), axis=1)


def _lambda(lq1, lk1, lq2, lk2, lam_init):
    return (jnp.exp(jnp.sum(lq1[...] * lk1[...], keepdims=True))
            - jnp.exp(jnp.sum(lq2[...] * lk2[...], keepdims=True)) + lam_init)


def _subln(o, gs, lam_init):
    ms = jnp.mean(o * o, axis=-1, keepdims=True)
    return o * lax.rsqrt(ms + SUBLN_EPS) * gs * (1.0 - lam_init)


def _attn_prompt_kernel(q_ref, k_ref, v_ref, lq1, lk1, lq2, lk2, gs_ref, o_ref, *, tq, d, lam_init):
    t = q_ref.shape[0]
    lam = _lambda(lq1, lk1, lq2, lk2, lam_init)
    gs = gs_ref[...]
    rq = lax.broadcasted_iota(jnp.int32, (tq, tq), 0) // CHUNK
    rk = lax.broadcasted_iota(jnp.int32, (tq, tq), 1) // CHUNK
    diag_mask = rk <= rq
    for i in range(t // tq):
        r0 = i * tq
        w_diag, w_prev = None, None
        for c in range(2):
            qc = q_ref[r0:r0 + tq, c * d:(c + 1) * d]
            s_diag = jnp.where(diag_mask, _dot_nt(qc, k_ref[r0:r0 + tq, c * d:(c + 1) * d]), NEG_INF)
            m = jnp.max(s_diag, axis=-1, keepdims=True)
            if i > 0:
                s_prev = _dot_nt(qc, k_ref[0:r0, c * d:(c + 1) * d])
                m = jnp.maximum(m, jnp.max(s_prev, axis=-1, keepdims=True))
            p_diag = jnp.exp(s_diag - m)
            l = jnp.sum(p_diag, axis=-1, keepdims=True)
            if i > 0:
                p_prev = jnp.exp(s_prev - m)
                l = l + jnp.sum(p_prev, axis=-1, keepdims=True)
            coef = 1.0 / l if c == 0 else -lam / l
            w_diag = p_diag * coef if c == 0 else w_diag + p_diag * coef
            if i > 0:
                w_prev = p_prev * coef if c == 0 else w_prev + p_prev * coef
        o = _dot(w_diag.astype(BF16), v_ref[r0:r0 + tq, :])
        if i > 0:
            o = o + _dot(w_prev.astype(BF16), v_ref[0:r0, :])
        o_ref[r0:r0 + tq, :] = _subln(o, gs, lam_init).astype(o_ref.dtype)


def _attn_prompt(q, k, v, lam_vecs, g_subln, n_heads, lam_init):
    b, t, width = q.shape
    hw = width // n_heads
    d = hw // 2
    tq = _tile(t, 256)
    assert tq % CHUNK == 0
    vec = pl.BlockSpec((1, d), lambda bi, h: (0, 0))
    seq = pl.BlockSpec((None, t, hw), lambda bi, h: (bi, 0, h))
    return pl.pallas_call(
        functools.partial(_attn_prompt_kernel, tq=tq, d=d, lam_init=lam_init),
        grid=(b, n_heads),
        in_specs=[seq, seq, seq, vec, vec, vec, vec, pl.BlockSpec((1, hw), lambda bi, h: (0, 0))],
        out_specs=seq,
        out_shape=jax.ShapeDtypeStruct((b, t, width), BF16),
        compiler_params=_params("parallel", "parallel"),
        name="attn_prompt",
    )(q, k, v, *lam_vecs, g_subln)


def _online_softmax_step(idx, s, v, m_sc, l_sc, acc_sc):
    m_old = m_sc[idx]
    m_new = jnp.maximum(m_old, jnp.max(s, axis=-1, keepdims=True))
    alpha = jnp.exp(m_old - m_new)
    p = jnp.exp(s - _rep(m_new, s.shape[1]))
    l_sc[idx] = alpha * l_sc[idx] + jnp.sum(p, axis=-1, keepdims=True)
    acc_sc[idx] = _rep(alpha, v.shape[1]) * acc_sc[idx] + _dot(p.astype(v.dtype), v)
    m_sc[idx] = m_new


def _attn_sample_kernel(q_ref, ck_ref, cv0_ref, cv1_ref, kn_ref, vn_ref, lq1, lk1, lq2, lk2, gs_ref, o_ref,
                        m_sc, l_sc, acc_sc, *, n_heads, d, tk, past, lam_init):
    kb = pl.program_id(1)
    nkb = pl.num_programs(1) - 1
    hw = 2 * d

    @pl.when(kb == 0)
    def _():
        m_sc[...] = jnp.full(m_sc.shape, -jnp.inf, F32)
        l_sc[...] = jnp.zeros(l_sc.shape, F32)
        acc_sc[...] = jnp.zeros(acc_sc.shape, F32)

    def update(k_of, v_of, mask):
        for h in range(n_heads):
            v = v_of(h)
            for c in range(2):
                s = _dot_nt(q_ref[:, h * hw + c * d:h * hw + (c + 1) * d], k_of(h, c))
                if mask is not None:
                    s = jnp.where(mask, s, NEG_INF)
                _online_softmax_step(2 * h + c, s, v, m_sc, l_sc, acc_sc)

    @pl.when(kb < nkb)
    def _():
        update(lambda h, c: ck_ref[pl.ds(2 * h + c, tk, stride=2 * n_heads), :].astype(BF16),
               lambda h: jnp.concatenate([cv0_ref[pl.ds(h, tk, stride=n_heads), :],
                                          cv1_ref[pl.ds(h, tk, stride=n_heads), :]], axis=1).astype(BF16),
               None)

    @pl.when(kb == nkb)
    def _():
        tq = q_ref.shape[0]
        rq = (past + lax.broadcasted_iota(jnp.int32, (tq, tq), 0)) // CHUNK
        rk = (past + lax.broadcasted_iota(jnp.int32, (tq, tq), 1)) // CHUNK
        update(lambda h, c: kn_ref[:, h * hw + c * d:h * hw + (c + 1) * d],
               lambda h: vn_ref[:, h * hw:(h + 1) * hw], rk <= rq)
        lam = _lambda(lq1, lk1, lq2, lk2, lam_init)
        for h in range(n_heads):
            o = (acc_sc[2 * h] / _rep(l_sc[2 * h], hw)
                 - lam * (acc_sc[2 * h + 1] / _rep(l_sc[2 * h + 1], hw)))
            o_ref[:, h * hw:(h + 1) * hw] = _subln(o, gs_ref[...], lam_init).astype(o_ref.dtype)


def _attn_sample(q, cache_k, cache_v, k_new, v_new, lam_vecs, g_subln, n_heads, lam_init):
    b, tq, width = q.shape
    hw = width // n_heads
    d = hw // 2
    assert d == STAT_LANES
    past = cache_v.shape[1] // n_heads
    tk = _tile(past, 512)
    nkb = past // tk
    vec = pl.BlockSpec((1, d), lambda bi, kb: (0, 0))
    new = pl.BlockSpec((None, tq, width), lambda bi, kb: (bi, 0, 0))
    ck_spec = pl.BlockSpec((None, tk * 2 * n_heads, d), lambda bi, kb: (bi, jnp.minimum(kb, nkb - 1), 0))
    cv_specs = [pl.BlockSpec((None, tk * n_heads, d), lambda bi, kb, c=c: (bi, jnp.minimum(kb, nkb - 1), c))
                for c in range(2)]
    return pl.pallas_call(
        functools.partial(_attn_sample_kernel, n_heads=n_heads, d=d, tk=tk, past=past, lam_init=lam_init),
        grid=(b, nkb + 1),
        in_specs=[new, ck_spec, *cv_specs, new, new, vec, vec, vec, vec,
                  pl.BlockSpec((1, hw), lambda bi, kb: (0, 0))],
        out_specs=new,
        out_shape=jax.ShapeDtypeStruct((b, tq, width), BF16),
        scratch_shapes=[
            pltpu.VMEM((2 * n_heads, tq, STAT_LANES), F32),
            pltpu.VMEM((2 * n_heads, tq, STAT_LANES), F32),
            pltpu.VMEM((2 * n_heads, tq, hw), F32),
        ],
        compiler_params=_params("parallel", "arbitrary"),
        name="attn_sample",
    )(q, cache_k, cache_v, cache_v, k_new, v_new, *lam_vecs, g_subln)


def _pool_mix_kernel(u_ref, prev_ref, hist_ref, wp_ref, sc_ref, o_ref, ext_ref, *, tt, pos0):
    t = pl.program_id(1)
    ext_ref[0:HIST_ROWS, :] = jnp.where(t == 0, hist_ref[...], prev_ref[...])
    ext_ref[HIST_ROWS:HIST_ROWS + tt, :] = u_ref[...]
    pos = pos0 + t * tt + lax.broadcasted_iota(jnp.int32, (tt, 1), 0)
    gw = u_ref.shape[1] // len(POOL_WINDOWS)
    for g, w in enumerate(POOL_WINDOWS):
        cols = slice(g * gw, (g + 1) * gw)
        win = ext_ref[HIST_ROWS:HIST_ROWS + tt, cols]
        for back in range(1, w):
            win = win + ext_ref[HIST_ROWS - back:HIST_ROWS - back + tt, cols]
        count = jnp.minimum(pos + 1, w).astype(F32)
        pooled = win / count - u_ref[:, cols]
        mixed = _dot(pooled.astype(BF16), wp_ref[g]) * sc_ref[:, cols]
        o_ref[:, cols] = mixed.astype(o_ref.dtype)


def _pool_mix(u, hist, w_pool, pool_scale, pos0):
    b, t, width = u.shape
    tt = _tile(t, 256)
    assert tt % HIST_ROWS == 0
    hist16 = jnp.pad(hist, ((0, 0), (HIST_ROWS - POOL_HIST, 0), (0, 0)))
    per = tt // HIST_ROWS
    return pl.pallas_call(
        functools.partial(_pool_mix_kernel, tt=tt, pos0=pos0),
        grid=(b, t // tt),
        in_specs=[
            pl.BlockSpec((None, tt, width), lambda bi, ti: (bi, ti, 0)),
            pl.BlockSpec((None, HIST_ROWS, width), lambda bi, ti: (bi, jnp.maximum(ti * per - 1, 0), 0)),
            pl.BlockSpec((None, HIST_ROWS, width), lambda bi, ti: (bi, 0, 0)),
            pl.BlockSpec(w_pool.shape, lambda bi, ti: (0, 0, 0)),
            pl.BlockSpec((1, width), lambda bi, ti: (0, 0)),
        ],
        out_specs=pl.BlockSpec((None, tt, width), lambda bi, ti: (bi, ti, 0)),
        out_shape=jax.ShapeDtypeStruct((b, t, width), BF16),
        scratch_shapes=[pltpu.VMEM((HIST_ROWS + tt, width), F32)],
        compiler_params=_params("parallel", "parallel"),
        name="pool_mix",
    )(u, u, hist16, w_pool, pool_scale)


def _trunk_layer(x, pos0, pool_hist, kv_cache, lam_init, w, wb):
    b, t, dm = x.shape
    m = b * t
    n_heads, d = w["n_heads"], w["head_dim"]
    qk_w = n_heads * 2 * d
    attn_w, pool_w = qk_w, w["w_pool_out"].shape[0]
    c_k, c_v, c_u, c_g = qk_w, 2 * qk_w, 2 * qk_w + attn_w, 2 * qk_w + attn_w + pool_w
    x2 = x.reshape(m, dm)
    w_in = w["w_in"]
    host = wb is None
    if host:
        wb = {"q": w_in[:, :qk_w].astype(BF16), "pool": w["w_pool"].astype(BF16)}

    def call(names, casts, *args):
        res = _tiled_call(*args, casts=casts if host else ())
        n_main = len(res) - (len(names) if host else 0)
        if host:
            wb.update(zip(names, res[n_main:]))
        return res[:n_main]

    xn = _rmsnorm_cast(x2, w["g_norm1"], NORM_EPS)
    (q,) = call(("k", "v", "u"), [(w_in, c_k, qk_w), (w_in, c_v, attn_w), (w_in, c_u, pool_w)],
                functools.partial(_q_body, scale=d ** -0.5), [xn], [(wb["q"], 0)], [], [w["g_q"]], [BF16], qk_w,
                1024, 1024, "proj_q")
    k_f, k_b = call(("attn_out",), [(w["w_attn_out"], 0, dm)],
                    _k_body, [xn], [(wb["k"], 0)], [], [w["g_k"]], [F32, BF16], qk_w, 1024, 1024, "proj_k")
    v_f, v_b = call(("pool_out",), [(w["w_pool_out"], 0, dm)],
                    _v_body, [xn], [(wb["v"], 0)], [], [], [F32, BF16], attn_w, 1024, 1024, "proj_v")
    (u,) = call(("g",), [(w_in, c_g, 2 * dm)],
                _u_body, [xn], [(wb["u"], 0)], [], [], [F32], pool_w, 1024, 512, "proj_u")
    (gates,) = call(("up", "o"), [(w["w_up"], 0, w["w_up"].shape[1]), (w["w_o"], 0, dm)],
                    _gate_body, [xn], [(wb["g"], 0)], [], [], [BF16], 2 * dm, 1024, 1024, "proj_gates")

    lam_vecs = w["lam_vecs"]
    if kv_cache is None:
        o = _attn_prompt(q.reshape(b, t, qk_w), k_b.reshape(b, t, qk_w), v_b.reshape(b, t, attn_w),
                         lam_vecs, w["g_subln"], n_heads, lam_init)
    else:
        ck, cv = kv_cache
        o = _attn_sample(q.reshape(b, t, qk_w), ck, cv, k_b.reshape(b, t, qk_w), v_b.reshape(b, t, attn_w),
                         lam_vecs, w["g_subln"], n_heads, lam_init)

    u3 = u.reshape(b, t, pool_w)
    yb_in = _pool_mix(u3, pool_hist, wb["pool"], w["pool_scale"], pos0)
    assert t >= POOL_HIST
    pool_state = u3[:, t - POOL_HIST:, :]

    (merged,) = _tiled_call(_merge_body, [o.reshape(m, attn_w), yb_in.reshape(m, pool_w)],
                            [(wb["attn_out"], 0), (wb["pool_out"], 0)], [(gates, 0), (gates, dm)], [],
                            [BF16], dm, 1024, 1024, "merge")
    (x1,) = _tiled_call(_residual_body, [merged], [(wb["o"], 0)], [(x2, 0)], [], [F32], dm, 1024, 1024,
                        "out_proj")
    hn = _rmsnorm_cast(x1, w["g_norm2"], NORM_EPS)
    (act,) = call(("down",), [(w["w_down"], 0, dm)],
                  _relu2_body, [hn], [(wb["up"], 0)], [], [], [BF16], w["w_up"].shape[1], 1024, 1024, "mlp_up")
    y = _down_proj(act, wb["down"], x1, 1024, 1024, 4096)
    return (y.reshape(b, t, dm), k_f.reshape(b, t, n_heads, 2, d), v_f.reshape(b, t, n_heads, 2 * d),
            pool_state), wb


def kernel(x_prompt, x_sample, cache_k, cache_v, state_pool, g_norm1, w_in, g_q, g_k, lambda_q1, lambda_k1,
           lambda_q2, lambda_k2, g_subln, w_attn_out, w_pool, pool_scale, w_pool_out, w_o, g_norm2, w_up,
           w_down):
    depth, dec_b, past, n_heads, _, d = cache_k.shape
    y_p, y_s = x_prompt, x_sample
    hist_p = jnp.zeros((x_prompt.shape[0], POOL_HIST, w_pool_out.shape[1]), F32)
    outs = [[] for _ in range(6)]
    for l in range(depth):
        lam_init = 0.8 - 0.6 * math.exp(-0.3 * l)
        row = lambda a: a[l].reshape(1, -1).astype(F32)
        w = dict(
            n_heads=n_heads, head_dim=d,
            g_norm1=g_norm1[l], g_norm2=g_norm2[l], g_q=row(g_q), g_k=row(g_k), g_subln=row(g_subln),
            lam_vecs=(row(lambda_q1), row(lambda_k1), row(lambda_q2), row(lambda_k2)),
            pool_scale=row(pool_scale),
            w_in=w_in[l], w_attn_out=w_attn_out[l], w_pool=w_pool[l], w_pool_out=w_pool_out[l], w_o=w_o[l],
            w_up=w_up[l], w_down=w_down[l],
        )
        (y_p, kp, vp, sp), wb = _trunk_layer(y_p, 0, hist_p, None, lam_init, w, None)
        kv = (cache_k[l].reshape(dec_b, past * n_heads * 2, d), cache_v[l].reshape(dec_b, past * n_heads, 2 * d))
        (y_s, ks, vs, ss), _ = _trunk_layer(y_s, past, state_pool[l], kv, lam_init, w, wb)
        for lst, val in zip(outs, (kp, vp, sp, ks, vs, ss)):
            lst.append(val)
    return (y_p, y_s) + tuple(jnp.stack(o) for o in outs)
```

```python
import functools
import math

import jax
import jax.numpy as jnp
from jax import lax
from jax.experimental import pallas as pl
from jax.experimental.pallas import tpu as pltpu

CHUNK = 64
POOL_WINDOWS = (2, 4, 8, 16)
POOL_HIST = max(POOL_WINDOWS) - 1
HIST_ROWS = 16
NORM_EPS = 1e-6
SUBLN_EPS = 1e-5
NEG_INF = -1e30
LOG2E = math.log2(math.e)
V7X_VMEM_LIMIT_BYTES = 56 * 1024 * 1024
LANES = 128
STAT_LANES = LANES
BF16_SUBLANES = 16

F32 = jnp.float32
BF16 = jnp.bfloat16


def _tile(dim, pref):
    t = min(dim, pref)
    while dim % t:
        t -= 1
    return t


def _params(*sem):
    return pltpu.CompilerParams(dimension_semantics=sem, vmem_limit_bytes=V7X_VMEM_LIMIT_BYTES)


def _dot(a, b):
    return jnp.dot(a, b, preferred_element_type=F32)


def _dot_nt(a, b):
    return lax.dot_general(a, b, (((1,), (1,)), ((), ())), preferred_element_type=F32)


def _rmsnorm_cast_kernel(x_ref, g_ref, o_ref, *, eps):
    x = x_ref[...]
    ms = jnp.mean(x * x, axis=-1, keepdims=True)
    o_ref[...] = (x * lax.rsqrt(ms + eps) * g_ref[...]).astype(o_ref.dtype)


def _rmsnorm_cast(x, g, eps):
    m, d = x.shape
    tm = _tile(m, 256)
    return pl.pallas_call(
        functools.partial(_rmsnorm_cast_kernel, eps=eps),
        grid=(m // tm,),
        in_specs=[pl.BlockSpec((tm, d), lambda i: (i, 0)), pl.BlockSpec((1, d), lambda i: (0, 0))],
        out_specs=pl.BlockSpec((tm, d), lambda i: (i, 0)),
        out_shape=jax.ShapeDtypeStruct((m, d), BF16),
        compiler_params=_params("parallel"),
        name="rmsnorm_cast",
    )(x, g.reshape(1, d).astype(F32))


def _with_casts(body, n_in, n_out, n_cast):
    def wrapped(*refs):
        main_in, cast_in = refs[:n_in], refs[n_in:n_in + n_cast]
        outs = refs[n_in + n_cast:]
        body(*main_in, *outs[:n_out])
        for src, dst in zip(cast_in, outs[n_out:]):
            dst[...] = src[...].astype(dst.dtype)
    return wrapped


def _tiled_call(body, rows, weights, tiles, vecs, out_dtypes, n_cols, tm, tn, name, casts=()):
    m = rows[0].shape[0]
    tm = _tile(m, tm)
    tn = _tile(n_cols, tn)
    steps = (m // tm) * (n_cols // tn)
    nj = n_cols // tn
    hosted = [c for c in casts if c[0].shape[0] % (steps * BF16_SUBLANES) == 0 and c[1] % c[2] == 0]
    if len(hosted) != len(casts):
        res = _tiled_call(body, rows, weights, tiles, vecs, out_dtypes, n_cols, tm, tn, name)
        return list(res) + [c[0][:, c[1]:c[1] + c[2]].astype(BF16) for c in casts]
    in_specs, args = [], []
    for a in rows:
        in_specs.append(pl.BlockSpec((tm, a.shape[1]), lambda i, j: (i, 0)))
        args.append(a)
    for w, c0 in weights:
        assert c0 % tn == 0
        in_specs.append(pl.BlockSpec((w.shape[0], tn), lambda i, j, o=c0 // tn: (0, j + o)))
        args.append(w)
    for t, c0 in tiles:
        assert c0 % tn == 0
        in_specs.append(pl.BlockSpec((tm, tn), lambda i, j, o=c0 // tn: (i, j + o)))
        args.append(t)
    for v in vecs:
        if isinstance(v, tuple):
            in_specs.append(pl.BlockSpec((1, tn), lambda i, j: (0, j)))
            args.append(v[1])
        else:
            in_specs.append(pl.BlockSpec(v.shape, lambda i, j: (0, 0)))
            args.append(v)
    n_in = len(args)
    out_specs, out_shape, j_sem = [], [], "parallel"
    for dt in out_dtypes:
        if not isinstance(dt, tuple):
            out_specs.append(pl.BlockSpec((tm, tn), lambda i, j: (i, j)))
            out_shape.append(jax.ShapeDtypeStruct((m, n_cols), dt))
        elif dt[0] == "rows":
            groups = n_cols // LANES
            out_specs.append(pl.BlockSpec((tm * groups, LANES), lambda i, j: (i, 0)))
            out_shape.append(jax.ShapeDtypeStruct((m * groups, LANES), dt[1]))
            j_sem = "arbitrary"
        else:
            out_specs.append(pl.BlockSpec((tm, LANES), lambda i, j: (i, j)))
            out_shape.append(jax.ShapeDtypeStruct((m, nj * LANES), dt[1]))
    for src, c0, n in casts:
        slab = src.shape[0] // steps
        in_specs.append(pl.BlockSpec((slab, n), lambda i, j, o=c0 // n: (i * nj + j, o)))
        args.append(src)
        out_specs.append(pl.BlockSpec((slab, n), lambda i, j: (i * nj + j, 0)))
        out_shape.append(jax.ShapeDtypeStruct((src.shape[0], n), BF16))
    if casts:
        body = _with_casts(body, n_in, len(out_dtypes), len(casts))
    return pl.pallas_call(
        body,
        grid=(m // tm, n_cols // tn),
        in_specs=in_specs,
        out_specs=out_specs,
        out_shape=out_shape,
        compiler_params=_params("parallel", j_sem),
        name=name,
    )(*args)


def _head_rmsnorm_store(acc, g, eps, scale, out_refs):
    group = g.shape[1]
    for c in range(acc.shape[1] // group):
        sl = slice(c * group, (c + 1) * group)
        blk = acc[:, sl]
        ms = jnp.mean(blk * blk, axis=-1, keepdims=True)
        y = blk * lax.rsqrt(ms + eps) * g
        for ref, s in zip(out_refs, scale):
            ref[:, sl] = (y if s == 1.0 else y * s).astype(ref.dtype)


def _q_body(a_ref, w_ref, g_ref, q_ref, *, scale):
    _head_rmsnorm_store(_dot(a_ref[...], w_ref[...]), g_ref[...], NORM_EPS, (scale,), (q_ref,))


def _k_body(a_ref, w_ref, g_ref, kf_ref, kb_ref, *, groups):
    acc = _dot(a_ref[...], w_ref[...])
    g = g_ref[...]
    tm, tn = acc.shape
    first = pl.program_id(1) * (tn // LANES)
    for c in range(tn // LANES):
        blk = acc[:, c * LANES:(c + 1) * LANES]
        ms = jnp.mean(blk * blk, axis=-1, keepdims=True)
        y = blk * lax.rsqrt(ms + NORM_EPS) * g
        kf_ref[pl.ds(first + c, tm, stride=groups), :] = y
        kb_ref[:, c * LANES:(c + 1) * LANES] = y.astype(kb_ref.dtype)


def _v_body(a_ref, w_ref, vf_ref, vb_ref):
    acc = _dot(a_ref[...], w_ref[...])
    vf_ref[...] = acc
    vb_ref[...] = acc.astype(vb_ref.dtype)


def _u_body(a_ref, w_ref, u_ref):
    u_ref[...] = _dot(a_ref[...], w_ref[...])


def _gate_body(a_ref, w_ref, g_ref):
    g_ref[...] = jax.nn.sigmoid(_dot(a_ref[...], w_ref[...])).astype(g_ref.dtype)


def _merge_body(o_ref, yb_ref, wa_ref, wp_ref, ga_ref, gb_ref, out_ref):
    ya = _dot(o_ref[...], wa_ref[...])
    yb = _dot(yb_ref[...], wp_ref[...])
    out_ref[...] = (ga_ref[...].astype(F32) * ya + gb_ref[...].astype(F32) * yb).astype(out_ref.dtype)


def _residual_norm_body(a_ref, w_ref, x_ref, g_ref, out_ref, xg_ref, ssq_ref):
    x1 = x_ref[...] + _dot(a_ref[...], w_ref[...])
    out_ref[...] = x1
    xg_ref[...] = (x1 * g_ref[...]).astype(xg_ref.dtype)
    ssq_ref[...] = jnp.broadcast_to(jnp.sum(x1 * x1, axis=-1, keepdims=True), ssq_ref.shape)


def _relu2_norm_body(a_ref, ssq_ref, w_ref, out_ref, *, dim):
    ssq = ssq_ref[...]
    tot = ssq[:, :LANES]
    for c in range(1, ssq.shape[1] // LANES):
        tot = tot + ssq[:, c * LANES:(c + 1) * LANES]
    r = lax.rsqrt(tot * (1.0 / dim) + NORM_EPS)
    acc = _dot(a_ref[...], w_ref[...])
    h = jnp.maximum(acc * _rep(r, acc.shape[1]), 0.0)
    out_ref[...] = (h * h).astype(out_ref.dtype)


def _down_body(a_ref, w_ref, x_ref, out_ref):
    @pl.when(pl.program_id(2) == 0)
    def _():
        out_ref[...] = x_ref[...]

    out_ref[...] += _dot(a_ref[...], w_ref[...])


def _down_proj(a, w, x, tm, tn, tk):
    m, kdim = a.shape
    n = w.shape[1]
    tm, tn, tk = _tile(m, tm), _tile(n, tn), _tile(kdim, tk)
    return pl.pallas_call(
        _down_body,
        grid=(m // tm, n // tn, kdim // tk),
        in_specs=[
            pl.BlockSpec((tm, tk), lambda i, j, k: (i, k)),
            pl.BlockSpec((tk, tn), lambda i, j, k: (k, j)),
            pl.BlockSpec((tm, tn), lambda i, j, k: (i, j)),
        ],
        out_specs=pl.BlockSpec((tm, tn), lambda i, j, k: (i, j)),
        out_shape=jax.ShapeDtypeStruct((m, n), F32),
        compiler_params=_params("parallel", "parallel", "arbitrary"),
        name="down_proj",
    )(a, w, x)


def _rep(x, width):
    if width <= STAT_LANES:
        return x[:, :width]
    return jnp.concatenate([x] * (width // STAT_LANES), axis=1)


def _lambda(lq1, lk1, lq2, lk2, lam_init):
    return (jnp.exp(jnp.sum(lq1[...] * lk1[...], keepdims=True))
            - jnp.exp(jnp.sum(lq2[...] * lk2[...], keepdims=True)) + lam_init)


def _subln(o, gs, lam_init):
    ms = jnp.mean(o * o, axis=-1, keepdims=True)
    return o * lax.rsqrt(ms + SUBLN_EPS) * gs * (1.0 - lam_init)


def _attn_prompt_kernel(q_ref, k_ref, v_ref, lq1, lk1, lq2, lk2, gs_ref, o_ref, *, tq, d, lam_init):
    t = q_ref.shape[0]
    lam = _lambda(lq1, lk1, lq2, lk2, lam_init)
    gs = gs_ref[...]
    rq = lax.broadcasted_iota(jnp.int32, (tq, tq), 0) // CHUNK
    rk = lax.broadcasted_iota(jnp.int32, (tq, tq), 1) // CHUNK
    diag_mask = rk <= rq
    for i in range(t // tq):
        r0 = i * tq
        w_diag, w_prev = None, None
        for c in range(2):
            qc = q_ref[r0:r0 + tq, c * d:(c + 1) * d]
            s_diag = jnp.where(diag_mask, _dot_nt(qc, k_ref[r0:r0 + tq, c * d:(c + 1) * d]), NEG_INF)
            m = jnp.max(s_diag, axis=-1, keepdims=True)
            if i > 0:
                s_prev = _dot_nt(qc, k_ref[0:r0, c * d:(c + 1) * d])
                m = jnp.maximum(m, jnp.max(s_prev, axis=-1, keepdims=True))
            p_diag = jnp.exp2(s_diag - m)
            l = jnp.sum(p_diag, axis=-1, keepdims=True)
            if i > 0:
                p_prev = jnp.exp2(s_prev - m)
                l = l + jnp.sum(p_prev, axis=-1, keepdims=True)
            coef = 1.0 / l if c == 0 else -lam / l
            w_diag = p_diag * coef if c == 0 else w_diag + p_diag * coef
            if i > 0:
                w_prev = p_prev * coef if c == 0 else w_prev + p_prev * coef
        o = _dot(w_diag.astype(BF16), v_ref[r0:r0 + tq, :])
        if i > 0:
            o = o + _dot(w_prev.astype(BF16), v_ref[0:r0, :])
        o_ref[r0:r0 + tq, :] = _subln(o, gs, lam_init).astype(o_ref.dtype)


def _attn_prompt(q, k, v, lam_vecs, g_subln, n_heads, lam_init):
    b, t, width = q.shape
    hw = width // n_heads
    d = hw // 2
    tq = _tile(t, 256)
    assert tq % CHUNK == 0
    vec = pl.BlockSpec((1, d), lambda bi, h: (0, 0))
    seq = pl.BlockSpec((None, t, hw), lambda bi, h: (bi, 0, h))
    return pl.pallas_call(
        functools.partial(_attn_prompt_kernel, tq=tq, d=d, lam_init=lam_init),
        grid=(b, n_heads),
        in_specs=[seq, seq, seq, vec, vec, vec, vec, pl.BlockSpec((1, hw), lambda bi, h: (0, 0))],
        out_specs=seq,
        out_shape=jax.ShapeDtypeStruct((b, t, width), BF16),
        compiler_params=_params("parallel", "parallel"),
        name="attn_prompt",
    )(q, k, v, *lam_vecs, g_subln)


def _online_softmax_step(idx, s, v, m_sc, l_sc, acc_sc):
    m_old = m_sc[idx]
    m_new = jnp.maximum(m_old, jnp.max(s, axis=-1, keepdims=True))
    alpha = jnp.exp2(m_old - m_new)
    p = jnp.exp2(s - _rep(m_new, s.shape[1]))
    l_sc[idx] = alpha * l_sc[idx] + jnp.sum(p, axis=-1, keepdims=True)
    acc_sc[idx] = _rep(alpha, v.shape[1]) * acc_sc[idx] + _dot(p.astype(v.dtype), v)
    m_sc[idx] = m_new


def _attn_sample_kernel(q_ref, ck_ref, cv0_ref, cv1_ref, kn_ref, vn_ref, lq1, lk1, lq2, lk2, gs_ref, o_ref,
                        m_sc, l_sc, acc_sc, *, n_heads, d, tk, past, lam_init):
    kb = pl.program_id(1)
    nkb = pl.num_programs(1) - 1
    hw = 2 * d

    @pl.when(kb == 0)
    def _():
        m_sc[...] = jnp.full(m_sc.shape, -jnp.inf, F32)
        l_sc[...] = jnp.zeros(l_sc.shape, F32)
        acc_sc[...] = jnp.zeros(acc_sc.shape, F32)

    def update(k_of, v_of, mask):
        for h in range(n_heads):
            v = v_of(h)
            for c in range(2):
                s = _dot_nt(q_ref[:, h * hw + c * d:h * hw + (c + 1) * d], k_of(h, c))
                if mask is not None:
                    s = jnp.where(mask, s, NEG_INF)
                _online_softmax_step(2 * h + c, s, v, m_sc, l_sc, acc_sc)

    @pl.when(kb < nkb)
    def _():
        update(lambda h, c: ck_ref[pl.ds(2 * h + c, tk, stride=2 * n_heads), :].astype(BF16),
               lambda h: jnp.concatenate([cv0_ref[pl.ds(h, tk, stride=n_heads), :],
                                          cv1_ref[pl.ds(h, tk, stride=n_heads), :]], axis=1).astype(BF16),
               None)

    @pl.when(kb == nkb)
    def _():
        tq = q_ref.shape[0]
        rq = (past + lax.broadcasted_iota(jnp.int32, (tq, tq), 0)) // CHUNK
        rk = (past + lax.broadcasted_iota(jnp.int32, (tq, tq), 1)) // CHUNK
        update(lambda h, c: kn_ref[:, h * hw + c * d:h * hw + (c + 1) * d],
               lambda h: vn_ref[:, h * hw:(h + 1) * hw], rk <= rq)
        lam = _lambda(lq1, lk1, lq2, lk2, lam_init)
        for h in range(n_heads):
            o = (acc_sc[2 * h] / _rep(l_sc[2 * h], hw)
                 - lam * (acc_sc[2 * h + 1] / _rep(l_sc[2 * h + 1], hw)))
            o_ref[:, h * hw:(h + 1) * hw] = _subln(o, gs_ref[...], lam_init).astype(o_ref.dtype)


def _attn_sample(q, cache_k, cache_v, k_new, v_new, lam_vecs, g_subln, n_heads, lam_init):
    b, tq, width = q.shape
    hw = width // n_heads
    d = hw // 2
    assert d == STAT_LANES
    past = cache_v.shape[1] // n_heads
    tk = _tile(past, 512)
    nkb = past // tk
    vec = pl.BlockSpec((1, d), lambda bi, kb: (0, 0))
    new = pl.BlockSpec((None, tq, width), lambda bi, kb: (bi, 0, 0))
    ck_spec = pl.BlockSpec((None, tk * 2 * n_heads, d), lambda bi, kb: (bi, jnp.minimum(kb, nkb - 1), 0))
    cv_specs = [pl.BlockSpec((None, tk * n_heads, d), lambda bi, kb, c=c: (bi, jnp.minimum(kb, nkb - 1), c))
                for c in range(2)]
    return pl.pallas_call(
        functools.partial(_attn_sample_kernel, n_heads=n_heads, d=d, tk=tk, past=past, lam_init=lam_init),
        grid=(b, nkb + 1),
        in_specs=[new, ck_spec, *cv_specs, new, new, vec, vec, vec, vec,
                  pl.BlockSpec((1, hw), lambda bi, kb: (0, 0))],
        out_specs=new,
        out_shape=jax.ShapeDtypeStruct((b, tq, width), BF16),
        scratch_shapes=[
            pltpu.VMEM((2 * n_heads, tq, STAT_LANES), F32),
            pltpu.VMEM((2 * n_heads, tq, STAT_LANES), F32),
            pltpu.VMEM((2 * n_heads, tq, hw), F32),
        ],
        compiler_params=_params("parallel", "arbitrary"),
        name="attn_sample",
    )(q, cache_k, cache_v, cache_v, k_new, v_new, *lam_vecs, g_subln)


def _pool_mix_kernel(u_ref, prev_ref, hist_ref, wp_ref, sc_ref, o_ref, ext_ref, *, tt, pos0):
    t = pl.program_id(1)
    ext_ref[0:HIST_ROWS, :] = jnp.where(t == 0, hist_ref[...], prev_ref[...])
    ext_ref[HIST_ROWS:HIST_ROWS + tt, :] = u_ref[...]
    pos = pos0 + t * tt + lax.broadcasted_iota(jnp.int32, (tt, 1), 0)
    gw = u_ref.shape[1] // len(POOL_WINDOWS)
    for g, w in enumerate(POOL_WINDOWS):
        cols = slice(g * gw, (g + 1) * gw)
        win = ext_ref[HIST_ROWS:HIST_ROWS + tt, cols]
        for back in range(1, w):
            win = win + ext_ref[HIST_ROWS - back:HIST_ROWS - back + tt, cols]
        count = jnp.minimum(pos + 1, w).astype(F32)
        pooled = win / count - u_ref[:, cols]
        mixed = _dot(pooled.astype(BF16), wp_ref[g]) * sc_ref[:, cols]
        o_ref[:, cols] = mixed.astype(o_ref.dtype)


def _pool_mix(u, hist, w_pool, pool_scale, pos0):
    b, t, width = u.shape
    tt = _tile(t, 256)
    assert tt % HIST_ROWS == 0
    hist16 = jnp.pad(hist, ((0, 0), (HIST_ROWS - POOL_HIST, 0), (0, 0)))
    per = tt // HIST_ROWS
    return pl.pallas_call(
        functools.partial(_pool_mix_kernel, tt=tt, pos0=pos0),
        grid=(b, t // tt),
        in_specs=[
            pl.BlockSpec((None, tt, width), lambda bi, ti: (bi, ti, 0)),
            pl.BlockSpec((None, HIST_ROWS, width), lambda bi, ti: (bi, jnp.maximum(ti * per - 1, 0), 0)),
            pl.BlockSpec((None, HIST_ROWS, width), lambda bi, ti: (bi, 0, 0)),
            pl.BlockSpec(w_pool.shape, lambda bi, ti: (0, 0, 0)),
            pl.BlockSpec((1, width), lambda bi, ti: (0, 0)),
        ],
        out_specs=pl.BlockSpec((None, tt, width), lambda bi, ti: (bi, ti, 0)),
        out_shape=jax.ShapeDtypeStruct((b, t, width), BF16),
        scratch_shapes=[pltpu.VMEM((HIST_ROWS + tt, width), F32)],
        compiler_params=_params("parallel", "parallel"),
        name="pool_mix",
    )(u, u, hist16, w_pool, pool_scale)


def _trunk_layer(x, pos0, pool_hist, kv_cache, lam_init, w, wb):
    b, t, dm = x.shape
    m = b * t
    n_heads, d = w["n_heads"], w["head_dim"]
    qk_w = n_heads * 2 * d
    attn_w, pool_w = qk_w, w["w_pool_out"].shape[0]
    c_k, c_v, c_u, c_g = qk_w, 2 * qk_w, 2 * qk_w + attn_w, 2 * qk_w + attn_w + pool_w
    x2 = x.reshape(m, dm)
    w_in = w["w_in"]
    host = wb is None
    if host:
        wb = {"q": w_in[:, :qk_w].astype(BF16), "pool": w["w_pool"].astype(BF16)}

    def call(names, casts, *args):
        res = _tiled_call(*args, casts=casts if host else ())
        n_main = len(res) - (len(names) if host else 0)
        if host:
            wb.update(zip(names, res[n_main:]))
        return res[:n_main]

    xn = _rmsnorm_cast(x2, w["g_norm1"], NORM_EPS)
    (q,) = call(("k", "v", "u"), [(w_in, c_k, qk_w), (w_in, c_v, attn_w), (w_in, c_u, pool_w)],
                functools.partial(_q_body, scale=d ** -0.5 * LOG2E), [xn], [(wb["q"], 0)], [], [w["g_q"]], [BF16], qk_w,
                1024, 1024, "proj_q")
    k_f, k_b = call(("attn_out",), [(w["w_attn_out"], 0, dm)],
                    functools.partial(_k_body, groups=qk_w // LANES), [xn], [(wb["k"], 0)], [], [w["g_k"]],
                    [("rows", F32), BF16], qk_w, 512, 1024, "proj_k")
    v_f, v_b = call(("pool_out",), [(w["w_pool_out"], 0, dm)],
                    _v_body, [xn], [(wb["v"], 0)], [], [], [F32, BF16], attn_w, 1024, 1024, "proj_v")
    (u,) = call(("g",), [(w_in, c_g, 2 * dm)],
                _u_body, [xn], [(wb["u"], 0)], [], [], [F32], pool_w, 1024, 512, "proj_u")
    (gates,) = call(("up", "o"), [(w["w_up"], 0, w["w_up"].shape[1]), (w["w_o"], 0, dm)],
                    _gate_body, [xn], [(wb["g"], 0)], [], [], [BF16], 2 * dm, 1024, 1024, "proj_gates")

    lam_vecs = w["lam_vecs"]
    if kv_cache is None:
        o = _attn_prompt(q.reshape(b, t, qk_w), k_b.reshape(b, t, qk_w), v_b.reshape(b, t, attn_w),
                         lam_vecs, w["g_subln"], n_heads, lam_init)
    else:
        ck, cv = kv_cache
        o = _attn_sample(q.reshape(b, t, qk_w), ck, cv, k_b.reshape(b, t, qk_w), v_b.reshape(b, t, attn_w),
                         lam_vecs, w["g_subln"], n_heads, lam_init)

    u3 = u.reshape(b, t, pool_w)
    yb_in = _pool_mix(u3, pool_hist, wb["pool"], w["pool_scale"], pos0)
    assert t >= POOL_HIST
    pool_state = u3[:, t - POOL_HIST:, :]

    (merged,) = _tiled_call(_merge_body, [o.reshape(m, attn_w), yb_in.reshape(m, pool_w)],
                            [(wb["attn_out"], 0), (wb["pool_out"], 0)], [(gates, 0), (gates, dm)], [],
                            [BF16], dm, 1024, 1024, "merge")
    x1, x1g, ssq = _tiled_call(_residual_norm_body, [merged], [(wb["o"], 0)], [(x2, 0)],
                               [("cols", w["g_norm2"])], [F32, BF16, ("stat", F32)], dm, 512, 1024, "out_proj")
    (act,) = call(("down",), [(w["w_down"], 0, dm)],
                  functools.partial(_relu2_norm_body, dim=dm), [x1g, ssq], [(wb["up"], 0)], [], [], [BF16],
                  w["w_up"].shape[1], 1024, 1024, "mlp_up")
    y = _down_proj(act, wb["down"], x1, 1024, 1024, 4096)
    return (y.reshape(b, t, dm), k_f.reshape(b, t, n_heads, 2, d), v_f.reshape(b, t, n_heads, 2 * d),
            pool_state), wb


def kernel(x_prompt, x_sample, cache_k, cache_v, state_pool, g_norm1, w_in, g_q, g_k, lambda_q1, lambda_k1,
           lambda_q2, lambda_k2, g_subln, w_attn_out, w_pool, pool_scale, w_pool_out, w_o, g_norm2, w_up,
           w_down):
    depth, dec_b, past, n_heads, _, d = cache_k.shape
    y_p, y_s = x_prompt, x_sample
    hist_p = jnp.zeros((x_prompt.shape[0], POOL_HIST, w_pool_out.shape[1]), F32)
    outs = [[] for _ in range(6)]
    for l in range(depth):
        lam_init = 0.8 - 0.6 * math.exp(-0.3 * l)
        row = lambda a: a[l].reshape(1, -1).astype(F32)
        w = dict(
            n_heads=n_heads, head_dim=d,
            g_norm1=g_norm1[l], g_norm2=row(g_norm2), g_q=row(g_q), g_k=row(g_k), g_subln=row(g_subln),
            lam_vecs=(row(lambda_q1), row(lambda_k1), row(lambda_q2), row(lambda_k2)),
            pool_scale=row(pool_scale),
            w_in=w_in[l], w_attn_out=w_attn_out[l], w_pool=w_pool[l], w_pool_out=w_pool_out[l], w_o=w_o[l],
            w_up=w_up[l], w_down=w_down[l],
        )
        (y_p, kp, vp, sp), wb = _trunk_layer(y_p, 0, hist_p, None, lam_init, w, None)
        kv = (cache_k[l].reshape(dec_b, past * n_heads * 2, d), cache_v[l].reshape(dec_b, past * n_heads, 2 * d))
        (y_s, ks, vs, ss), _ = _trunk_layer(y_s, past, state_pool[l], kv, lam_init, w, wb)
        for lst, val in zip(outs, (kp, vp, sp, ks, vs, ss)):
            lst.append(val)
    return (y_p, y_s) + tuple(jnp.stack(o) for o in outs)
```

```python
import functools
import math

import jax
import jax.numpy as jnp
from jax import lax
from jax.experimental import pallas as pl
from jax.experimental.pallas import tpu as pltpu

CHUNK = 64
POOL_WINDOWS = (2, 4, 8, 16)
POOL_HIST = max(POOL_WINDOWS) - 1
HIST_ROWS = 16
NORM_EPS = 1e-6
SUBLN_EPS = 1e-5
NEG_INF = -1e30
LOG2E = math.log2(math.e)
V7X_VMEM_LIMIT_BYTES = 56 * 1024 * 1024
LANES = 128
STAT_LANES = LANES
BF16_SUBLANES = 16

F32 = jnp.float32
BF16 = jnp.bfloat16


def _tile(dim, pref):
    t = min(dim, pref)
    while dim % t:
        t -= 1
    return t


def _params(*sem):
    return pltpu.CompilerParams(dimension_semantics=sem, vmem_limit_bytes=V7X_VMEM_LIMIT_BYTES)


def _dot(a, b):
    return jnp.dot(a, b, preferred_element_type=F32)


def _dot_nt(a, b):
    return lax.dot_general(a, b, (((1,), (1,)), ((), ())), preferred_element_type=F32)


def _rmsnorm_cast_kernel(x_ref, g_ref, o_ref, *, eps):
    x = x_ref[...]
    ms = jnp.mean(x * x, axis=-1, keepdims=True)
    o_ref[...] = (x * lax.rsqrt(ms + eps) * g_ref[...]).astype(o_ref.dtype)


def _rmsnorm_cast(x, g, eps):
    m, d = x.shape
    tm = _tile(m, 256)
    return pl.pallas_call(
        functools.partial(_rmsnorm_cast_kernel, eps=eps),
        grid=(m // tm,),
        in_specs=[pl.BlockSpec((tm, d), lambda i: (i, 0)), pl.BlockSpec((1, d), lambda i: (0, 0))],
        out_specs=pl.BlockSpec((tm, d), lambda i: (i, 0)),
        out_shape=jax.ShapeDtypeStruct((m, d), BF16),
        compiler_params=_params("parallel"),
        name="rmsnorm_cast",
    )(x, g.reshape(1, d).astype(F32))


def _with_casts(body, n_in, n_out, n_cast):
    def wrapped(*refs):
        main_in, cast_in = refs[:n_in], refs[n_in:n_in + n_cast]
        outs = refs[n_in + n_cast:]
        body(*main_in, *outs[:n_out])
        for src, dst in zip(cast_in, outs[n_out:]):
            dst[...] = src[...].astype(dst.dtype)
    return wrapped


def _tiled_call(body, rows, weights, tiles, vecs, out_dtypes, n_cols, tm, tn, name, casts=()):
    m = rows[0].shape[0]
    tm = _tile(m, tm)
    tn = _tile(n_cols, tn)
    steps = (m // tm) * (n_cols // tn)
    nj = n_cols // tn
    hosted = [c for c in casts if c[0].shape[0] % (steps * BF16_SUBLANES) == 0 and c[1] % c[2] == 0]
    if len(hosted) != len(casts):
        res = _tiled_call(body, rows, weights, tiles, vecs, out_dtypes, n_cols, tm, tn, name)
        return list(res) + [c[0][:, c[1]:c[1] + c[2]].astype(BF16) for c in casts]
    in_specs, args = [], []
    for a in rows:
        in_specs.append(pl.BlockSpec((tm, a.shape[1]), lambda i, j: (i, 0)))
        args.append(a)
    for w, c0 in weights:
        assert c0 % tn == 0
        in_specs.append(pl.BlockSpec((w.shape[0], tn), lambda i, j, o=c0 // tn: (0, j + o)))
        args.append(w)
    for t, c0 in tiles:
        assert c0 % tn == 0
        in_specs.append(pl.BlockSpec((tm, tn), lambda i, j, o=c0 // tn: (i, j + o)))
        args.append(t)
    for v in vecs:
        if isinstance(v, tuple):
            in_specs.append(pl.BlockSpec((1, tn), lambda i, j: (0, j)))
            args.append(v[1])
        else:
            in_specs.append(pl.BlockSpec(v.shape, lambda i, j: (0, 0)))
            args.append(v)
    n_in = len(args)
    out_specs, out_shape, j_sem = [], [], "parallel"
    for dt in out_dtypes:
        if not isinstance(dt, tuple):
            out_specs.append(pl.BlockSpec((tm, tn), lambda i, j: (i, j)))
            out_shape.append(jax.ShapeDtypeStruct((m, n_cols), dt))
        elif dt[0] == "rows":
            groups = n_cols // LANES
            out_specs.append(pl.BlockSpec((tm * groups, LANES), lambda i, j: (i, 0)))
            out_shape.append(jax.ShapeDtypeStruct((m * groups, LANES), dt[1]))
            j_sem = "arbitrary"
        else:
            out_specs.append(pl.BlockSpec((tm, LANES), lambda i, j: (i, j)))
            out_shape.append(jax.ShapeDtypeStruct((m, nj * LANES), dt[1]))
    for src, c0, n in casts:
        slab = src.shape[0] // steps
        in_specs.append(pl.BlockSpec((slab, n), lambda i, j, o=c0 // n: (i * nj + j, o)))
        args.append(src)
        out_specs.append(pl.BlockSpec((slab, n), lambda i, j: (i * nj + j, 0)))
        out_shape.append(jax.ShapeDtypeStruct((src.shape[0], n), BF16))
    if casts:
        body = _with_casts(body, n_in, len(out_dtypes), len(casts))
    return pl.pallas_call(
        body,
        grid=(m // tm, n_cols // tn),
        in_specs=in_specs,
        out_specs=out_specs,
        out_shape=out_shape,
        compiler_params=_params("parallel", j_sem),
        name=name,
    )(*args)


def _head_rmsnorm_store(acc, g, eps, scale, out_refs):
    group = g.shape[1]
    for c in range(acc.shape[1] // group):
        sl = slice(c * group, (c + 1) * group)
        blk = acc[:, sl]
        ms = jnp.mean(blk * blk, axis=-1, keepdims=True)
        y = blk * lax.rsqrt(ms + eps) * g
        for ref, s in zip(out_refs, scale):
            ref[:, sl] = (y if s == 1.0 else y * s).astype(ref.dtype)


def _q_body(a_ref, w_ref, g_ref, q_ref, *, scale):
    _head_rmsnorm_store(_dot(a_ref[...], w_ref[...]), g_ref[...], NORM_EPS, (scale,), (q_ref,))


def _k_body(a_ref, w_ref, g_ref, kf_ref, kb_ref, *, groups):
    acc = _dot(a_ref[...], w_ref[...])
    g = g_ref[...]
    tm, tn = acc.shape
    first = pl.program_id(1) * (tn // LANES)
    for c in range(tn // LANES):
        blk = acc[:, c * LANES:(c + 1) * LANES]
        ms = jnp.mean(blk * blk, axis=-1, keepdims=True)
        y = blk * lax.rsqrt(ms + NORM_EPS) * g
        kf_ref[pl.ds(first + c, tm, stride=groups), :] = y
        kb_ref[:, c * LANES:(c + 1) * LANES] = y.astype(kb_ref.dtype)


def _v_body(a_ref, w_ref, vf_ref, vb_ref):
    acc = _dot(a_ref[...], w_ref[...])
    vf_ref[...] = acc
    vb_ref[...] = acc.astype(vb_ref.dtype)


def _u_body(a_ref, w_ref, u_ref):
    u_ref[...] = _dot(a_ref[...], w_ref[...])


def _gate_body(a_ref, w_ref, g_ref):
    g_ref[...] = jax.nn.sigmoid(_dot(a_ref[...], w_ref[...])).astype(g_ref.dtype)


def _merge_body(o_ref, yb_ref, wa_ref, wp_ref, ga_ref, gb_ref, out_ref):
    ya = _dot(o_ref[...], wa_ref[...])
    yb = _dot(yb_ref[...], wp_ref[...])
    out_ref[...] = (ga_ref[...].astype(F32) * ya + gb_ref[...].astype(F32) * yb).astype(out_ref.dtype)


def _residual_norm_body(a_ref, w_ref, x_ref, g_ref, out_ref, xg_ref, ssq_ref):
    x1 = x_ref[...] + _dot(a_ref[...], w_ref[...])
    out_ref[...] = x1
    xg_ref[...] = (x1 * g_ref[...]).astype(xg_ref.dtype)
    ssq_ref[...] = jnp.broadcast_to(jnp.sum(x1 * x1, axis=-1, keepdims=True), ssq_ref.shape)


def _relu2_norm_body(a_ref, ssq_ref, w_ref, out_ref, *, dim):
    ssq = ssq_ref[...]
    tot = ssq[:, :LANES]
    for c in range(1, ssq.shape[1] // LANES):
        tot = tot + ssq[:, c * LANES:(c + 1) * LANES]
    r = lax.rsqrt(tot * (1.0 / dim) + NORM_EPS)
    acc = _dot(a_ref[...], w_ref[...])
    h = jnp.maximum(acc * _rep(r, acc.shape[1]), 0.0)
    out_ref[...] = (h * h).astype(out_ref.dtype)


def _down_body(a_ref, w_ref, x_ref, out_ref):
    @pl.when(pl.program_id(2) == 0)
    def _():
        out_ref[...] = x_ref[...]

    out_ref[...] += _dot(a_ref[...], w_ref[...])


def _down_proj(a, w, x, tm, tn, tk):
    m, kdim = a.shape
    n = w.shape[1]
    tm, tn, tk = _tile(m, tm), _tile(n, tn), _tile(kdim, tk)
    return pl.pallas_call(
        _down_body,
        grid=(m // tm, n // tn, kdim // tk),
        in_specs=[
            pl.BlockSpec((tm, tk), lambda i, j, k: (i, k)),
            pl.BlockSpec((tk, tn), lambda i, j, k: (k, j)),
            pl.BlockSpec((tm, tn), lambda i, j, k: (i, j)),
        ],
        out_specs=pl.BlockSpec((tm, tn), lambda i, j, k: (i, j)),
        out_shape=jax.ShapeDtypeStruct((m, n), F32),
        compiler_params=_params("parallel", "parallel", "arbitrary"),
        name="down_proj",
    )(a, w, x)


def _rep(x, width):
    if width <= STAT_LANES:
        return x[:, :width]
    return jnp.concatenate([x] * (width // STAT_LANES), axis=1)


def _lambda(lq1, lk1, lq2, lk2, lam_init):
    return (jnp.exp(jnp.sum(lq1[...] * lk1[...], keepdims=True))
            - jnp.exp(jnp.sum(lq2[...] * lk2[...], keepdims=True)) + lam_init)


def _subln(o, gs, lam_init):
    ms = jnp.mean(o * o, axis=-1, keepdims=True)
    return o * lax.rsqrt(ms + SUBLN_EPS) * gs * (1.0 - lam_init)


def _attn_prompt_kernel(q_ref, k_ref, v_ref, lq1, lk1, lq2, lk2, gs_ref, o_ref, *, tq, d, lam_init):
    t = q_ref.shape[0]
    lam = _lambda(lq1, lk1, lq2, lk2, lam_init)
    gs = gs_ref[...]
    rq = lax.broadcasted_iota(jnp.int32, (tq, tq), 0) // CHUNK
    rk = lax.broadcasted_iota(jnp.int32, (tq, tq), 1) // CHUNK
    diag_mask = rk <= rq
    for i in range(t // tq):
        r0 = i * tq
        w_diag, w_prev = None, None
        for c in range(2):
            qc = q_ref[r0:r0 + tq, c * d:(c + 1) * d]
            s_diag = jnp.where(diag_mask, _dot_nt(qc, k_ref[r0:r0 + tq, c * d:(c + 1) * d]), NEG_INF)
            m = jnp.max(s_diag, axis=-1, keepdims=True)
            if i > 0:
                s_prev = _dot_nt(qc, k_ref[0:r0, c * d:(c + 1) * d])
                m = jnp.maximum(m, jnp.max(s_prev, axis=-1, keepdims=True))
            p_diag = jnp.exp2(s_diag - m)
            l = jnp.sum(p_diag, axis=-1, keepdims=True)
            if i > 0:
                p_prev = jnp.exp2(s_prev - m)
                l = l + jnp.sum(p_prev, axis=-1, keepdims=True)
            if c == 0:
                l1, w_diag = l, p_diag
                if i > 0:
                    w_prev = p_prev
            else:
                ratio = -lam * l1 / l
                w_diag = w_diag + p_diag * ratio
                if i > 0:
                    w_prev = w_prev + p_prev * ratio
        o = _dot(w_diag.astype(BF16), v_ref[r0:r0 + tq, :])
        if i > 0:
            o = o + _dot(w_prev.astype(BF16), v_ref[0:r0, :])
        o_ref[r0:r0 + tq, :] = _subln(o / l1, gs, lam_init).astype(o_ref.dtype)


def _attn_prompt(q, k, v, lam_vecs, g_subln, n_heads, lam_init):
    b, t, width = q.shape
    hw = width // n_heads
    d = hw // 2
    tq = _tile(t, 256)
    assert tq % CHUNK == 0
    vec = pl.BlockSpec((1, d), lambda bi, h: (0, 0))
    seq = pl.BlockSpec((None, t, hw), lambda bi, h: (bi, 0, h))
    return pl.pallas_call(
        functools.partial(_attn_prompt_kernel, tq=tq, d=d, lam_init=lam_init),
        grid=(b, n_heads),
        in_specs=[seq, seq, seq, vec, vec, vec, vec, pl.BlockSpec((1, hw), lambda bi, h: (0, 0))],
        out_specs=seq,
        out_shape=jax.ShapeDtypeStruct((b, t, width), BF16),
        compiler_params=_params("parallel", "parallel"),
        name="attn_prompt",
    )(q, k, v, *lam_vecs, g_subln)


def _attn_sample_kernel(q_ref, ck_ref, cv0_ref, cv1_ref, kn_ref, vn_ref, lq1, lk1, lq2, lk2, gs_ref, o_ref,
                        s_sc, p_sc, m_sc, l_sc, acc_sc, *, n_heads, d, tk, past, lam_init):
    kb = pl.program_id(1)
    nkb = pl.num_programs(1) - 1
    hw = 2 * d
    tq = q_ref.shape[0]

    @pl.when(kb == 0)
    def _():
        m_sc[...] = jnp.full(m_sc.shape, -jnp.inf, F32)
        l_sc[...] = jnp.zeros(l_sc.shape, F32)
        acc_sc[...] = jnp.zeros(acc_sc.shape, F32)

    def update(k_of, v_of, mask, width):
        for h in range(n_heads):
            for c in range(2):
                s = _dot_nt(q_ref[:, h * hw + c * d:h * hw + (c + 1) * d], k_of(h, c))
                if mask is not None:
                    s = jnp.where(mask, s, NEG_INF)
                s_sc[(2 * h + c) * tq:(2 * h + c + 1) * tq, :width] = s
        s = s_sc[:, :width]
        m_old = m_sc[...]
        m_new = jnp.maximum(m_old, jnp.max(s, axis=-1, keepdims=True))
        alpha = jnp.exp2(m_old - m_new)
        p = jnp.exp2(s - _rep(m_new, width))
        l_sc[...] = alpha * l_sc[...] + jnp.sum(p, axis=-1, keepdims=True)
        m_sc[...] = m_new
        p_sc[:, :width] = p.astype(p_sc.dtype)
        for h in range(n_heads):
            rows = slice(2 * h * tq, (2 * h + 2) * tq)
            pv = _dot(p_sc[rows, :width], v_of(h))
            acc_sc[rows, :] = _rep(alpha[rows], hw) * acc_sc[rows, :] + pv

    @pl.when(kb < nkb)
    def _():
        update(lambda h, c: ck_ref[pl.ds(2 * h + c, tk, stride=2 * n_heads), :].astype(BF16),
               lambda h: jnp.concatenate([cv0_ref[pl.ds(h, tk, stride=n_heads), :],
                                          cv1_ref[pl.ds(h, tk, stride=n_heads), :]], axis=1).astype(BF16),
               None, tk)

    @pl.when(kb == nkb)
    def _():
        rq = (past + lax.broadcasted_iota(jnp.int32, (tq, tq), 0)) // CHUNK
        rk = (past + lax.broadcasted_iota(jnp.int32, (tq, tq), 1)) // CHUNK
        update(lambda h, c: kn_ref[:, h * hw + c * d:h * hw + (c + 1) * d],
               lambda h: vn_ref[:, h * hw:(h + 1) * hw], rk <= rq, tq)
        lam = _lambda(lq1, lk1, lq2, lk2, lam_init)
        for h in range(n_heads):
            r1, r2 = slice(2 * h * tq, (2 * h + 1) * tq), slice((2 * h + 1) * tq, (2 * h + 2) * tq)
            o = acc_sc[r1, :] / _rep(l_sc[r1, :], hw) - lam * (acc_sc[r2, :] / _rep(l_sc[r2, :], hw))
            o_ref[:, h * hw:(h + 1) * hw] = _subln(o, gs_ref[...], lam_init).astype(o_ref.dtype)


def _attn_sample(q, cache_k, cache_v, k_new, v_new, lam_vecs, g_subln, n_heads, lam_init):
    b, tq, width = q.shape
    hw = width // n_heads
    d = hw // 2
    assert d == STAT_LANES
    past = cache_v.shape[1] // n_heads
    tk = _tile(past, 512)
    nkb = past // tk
    vec = pl.BlockSpec((1, d), lambda bi, kb: (0, 0))
    new = pl.BlockSpec((None, tq, width), lambda bi, kb: (bi, 0, 0))
    ck_spec = pl.BlockSpec((None, tk * 2 * n_heads, d), lambda bi, kb: (bi, jnp.minimum(kb, nkb - 1), 0))
    cv_specs = [pl.BlockSpec((None, tk * n_heads, d), lambda bi, kb, c=c: (bi, jnp.minimum(kb, nkb - 1), c))
                for c in range(2)]
    return pl.pallas_call(
        functools.partial(_attn_sample_kernel, n_heads=n_heads, d=d, tk=tk, past=past, lam_init=lam_init),
        grid=(b, nkb + 1),
        in_specs=[new, ck_spec, *cv_specs, new, new, vec, vec, vec, vec,
                  pl.BlockSpec((1, hw), lambda bi, kb: (0, 0))],
        out_specs=new,
        out_shape=jax.ShapeDtypeStruct((b, tq, width), BF16),
        scratch_shapes=[
            pltpu.VMEM((2 * n_heads * tq, tk), F32),
            pltpu.VMEM((2 * n_heads * tq, tk), BF16),
            pltpu.VMEM((2 * n_heads * tq, STAT_LANES), F32),
            pltpu.VMEM((2 * n_heads * tq, STAT_LANES), F32),
            pltpu.VMEM((2 * n_heads * tq, hw), F32),
        ],
        compiler_params=_params("parallel", "arbitrary"),
        name="attn_sample",
    )(q, cache_k, cache_v, cache_v, k_new, v_new, *lam_vecs, g_subln)


def _pool_mix_kernel(u_ref, prev_ref, hist_ref, wp_ref, sc_ref, o_ref, ext_ref, *, tt, pos0):
    t = pl.program_id(1)
    ext_ref[0:HIST_ROWS, :] = jnp.where(t == 0, hist_ref[...], prev_ref[...])
    ext_ref[HIST_ROWS:HIST_ROWS + tt, :] = u_ref[...]
    pos = pos0 + t * tt + lax.broadcasted_iota(jnp.int32, (tt, 1), 0)
    gw = u_ref.shape[1] // len(POOL_WINDOWS)
    for g, w in enumerate(POOL_WINDOWS):
        cols = slice(g * gw, (g + 1) * gw)
        win = ext_ref[HIST_ROWS:HIST_ROWS + tt, cols]
        for back in range(1, w):
            win = win + ext_ref[HIST_ROWS - back:HIST_ROWS - back + tt, cols]
        count = jnp.minimum(pos + 1, w).astype(F32)
        pooled = win / count - u_ref[:, cols]
        mixed = _dot(pooled.astype(BF16), wp_ref[g]) * sc_ref[:, cols]
        o_ref[:, cols] = mixed.astype(o_ref.dtype)


def _pool_mix(u, hist, w_pool, pool_scale, pos0):
    b, t, width = u.shape
    tt = _tile(t, 256)
    assert tt % HIST_ROWS == 0
    hist16 = jnp.pad(hist, ((0, 0), (HIST_ROWS - POOL_HIST, 0), (0, 0)))
    per = tt // HIST_ROWS
    return pl.pallas_call(
        functools.partial(_pool_mix_kernel, tt=tt, pos0=pos0),
        grid=(b, t // tt),
        in_specs=[
            pl.BlockSpec((None, tt, width), lambda bi, ti: (bi, ti, 0)),
            pl.BlockSpec((None, HIST_ROWS, width), lambda bi, ti: (bi, jnp.maximum(ti * per - 1, 0), 0)),
            pl.BlockSpec((None, HIST_ROWS, width), lambda bi, ti: (bi, 0, 0)),
            pl.BlockSpec(w_pool.shape, lambda bi, ti: (0, 0, 0)),
            pl.BlockSpec((1, width), lambda bi, ti: (0, 0)),
        ],
        out_specs=pl.BlockSpec((None, tt, width), lambda bi, ti: (bi, ti, 0)),
        out_shape=jax.ShapeDtypeStruct((b, t, width), BF16),
        scratch_shapes=[pltpu.VMEM((HIST_ROWS + tt, width), F32)],
        compiler_params=_params("parallel", "parallel"),
        name="pool_mix",
    )(u, u, hist16, w_pool, pool_scale)


def _trunk_layer(x, pos0, pool_hist, kv_cache, lam_init, w, wb):
    b, t, dm = x.shape
    m = b * t
    n_heads, d = w["n_heads"], w["head_dim"]
    qk_w = n_heads * 2 * d
    attn_w, pool_w = qk_w, w["w_pool_out"].shape[0]
    c_k, c_v, c_u, c_g = qk_w, 2 * qk_w, 2 * qk_w + attn_w, 2 * qk_w + attn_w + pool_w
    x2 = x.reshape(m, dm)
    w_in = w["w_in"]
    host = wb is None
    if host:
        wb = {"q": w_in[:, :qk_w].astype(BF16), "pool": w["w_pool"].astype(BF16)}

    def call(names, casts, *args):
        res = _tiled_call(*args, casts=casts if host else ())
        n_main = len(res) - (len(names) if host else 0)
        if host:
            wb.update(zip(names, res[n_main:]))
        return res[:n_main]

    xn = _rmsnorm_cast(x2, w["g_norm1"], NORM_EPS)
    (q,) = call(("k", "v", "u"), [(w_in, c_k, qk_w), (w_in, c_v, attn_w), (w_in, c_u, pool_w)],
                functools.partial(_q_body, scale=d ** -0.5 * LOG2E), [xn], [(wb["q"], 0)], [], [w["g_q"]], [BF16], qk_w,
                1024, 1024, "proj_q")
    k_f, k_b = call(("attn_out", "g"), [(w["w_attn_out"], 0, dm), (w_in, c_g, 2 * dm)],
                    functools.partial(_k_body, groups=qk_w // LANES), [xn], [(wb["k"], 0)], [], [w["g_k"]],
                    [("rows", F32), BF16], qk_w, 512, 1024, "proj_k")
    v_f, v_b = call(("pool_out",), [(w["w_pool_out"], 0, dm)],
                    _v_body, [xn], [(wb["v"], 0)], [], [], [F32, BF16], attn_w, 1024, 1024, "proj_v")
    (u,) = _tiled_call(_u_body, [xn], [(wb["u"], 0)], [], [], [F32], pool_w, 1024, 1024, "proj_u")
    (gates,) = call(("up", "o"), [(w["w_up"], 0, w["w_up"].shape[1]), (w["w_o"], 0, dm)],
                    _gate_body, [xn], [(wb["g"], 0)], [], [], [BF16], 2 * dm, 1024, 1024, "proj_gates")

    lam_vecs = w["lam_vecs"]
    if kv_cache is None:
        o = _attn_prompt(q.reshape(b, t, qk_w), k_b.reshape(b, t, qk_w), v_b.reshape(b, t, attn_w),
                         lam_vecs, w["g_subln"], n_heads, lam_init)
    else:
        ck, cv = kv_cache
        o = _attn_sample(q.reshape(b, t, qk_w), ck, cv, k_b.reshape(b, t, qk_w), v_b.reshape(b, t, attn_w),
                         lam_vecs, w["g_subln"], n_heads, lam_init)

    u3 = u.reshape(b, t, pool_w)
    yb_in = _pool_mix(u3, pool_hist, wb["pool"], w["pool_scale"], pos0)
    assert t >= POOL_HIST
    pool_state = u3[:, t - POOL_HIST:, :]

    (merged,) = _tiled_call(_merge_body, [o.reshape(m, attn_w), yb_in.reshape(m, pool_w)],
                            [(wb["attn_out"], 0), (wb["pool_out"], 0)], [(gates, 0), (gates, dm)], [],
                            [BF16], dm, 1024, 1024, "merge")
    x1, x1g, ssq = _tiled_call(_residual_norm_body, [merged], [(wb["o"], 0)], [(x2, 0)],
                               [("cols", w["g_norm2"])], [F32, BF16, ("stat", F32)], dm, 512, 1024, "out_proj")
    (act,) = call(("down",), [(w["w_down"], 0, dm)],
                  functools.partial(_relu2_norm_body, dim=dm), [x1g, ssq], [(wb["up"], 0)], [], [], [BF16],
                  w["w_up"].shape[1], 1024, 1024, "mlp_up")
    y = _down_proj(act, wb["down"], x1, 1024, 1024, 4096)
    return (y.reshape(b, t, dm), k_f.reshape(b, t, n_heads, 2, d), v_f.reshape(b, t, n_heads, 2 * d),
            pool_state), wb


def kernel(x_prompt, x_sample, cache_k, cache_v, state_pool, g_norm1, w_in, g_q, g_k, lambda_q1, lambda_k1,
           lambda_q2, lambda_k2, g_subln, w_attn_out, w_pool, pool_scale, w_pool_out, w_o, g_norm2, w_up,
           w_down):
    depth, dec_b, past, n_heads, _, d = cache_k.shape
    y_p, y_s = x_prompt, x_sample
    hist_p = jnp.zeros((x_prompt.shape[0], POOL_HIST, w_pool_out.shape[1]), F32)
    outs = [[] for _ in range(6)]
    for l in range(depth):
        lam_init = 0.8 - 0.6 * math.exp(-0.3 * l)
        row = lambda a: a[l].reshape(1, -1).astype(F32)
        w = dict(
            n_heads=n_heads, head_dim=d,
            g_norm1=g_norm1[l], g_norm2=row(g_norm2), g_q=row(g_q), g_k=row(g_k), g_subln=row(g_subln),
            lam_vecs=(row(lambda_q1), row(lambda_k1), row(lambda_q2), row(lambda_k2)),
            pool_scale=row(pool_scale),
            w_in=w_in[l], w_attn_out=w_attn_out[l], w_pool=w_pool[l], w_pool_out=w_pool_out[l], w_o=w_o[l],
            w_up=w_up[l], w_down=w_down[l],
        )
        (y_p, kp, vp, sp), wb = _trunk_layer(y_p, 0, hist_p, None, lam_init, w, None)
        kv = (cache_k[l].reshape(dec_b, past * n_heads * 2, d), cache_v[l].reshape(dec_b, past * n_heads, 2 * d))
        (y_s, ks, vs, ss), _ = _trunk_layer(y_s, past, state_pool[l], kv, lam_init, w, wb)
        for lst, val in zip(outs, (kp, vp, sp, ks, vs, ss)):
            lst.append(val)
    return (y_p, y_s) + tuple(jnp.stack(o) for o in outs)
```

```python
import functools
import math

import jax
import jax.numpy as jnp
from jax import lax
from jax.experimental import pallas as pl
from jax.experimental.pallas import tpu as pltpu

CHUNK = 64
POOL_WINDOWS = (2, 4, 8, 16)
POOL_HIST = max(POOL_WINDOWS) - 1
HIST_ROWS = 16
NORM_EPS = 1e-6
SUBLN_EPS = 1e-5
NEG_INF = -1e30
LOG2E = math.log2(math.e)
V7X_VMEM_LIMIT_BYTES = 56 * 1024 * 1024
LANES = 128
STAT_LANES = LANES
BF16_SUBLANES = 16

F32 = jnp.float32
BF16 = jnp.bfloat16


def _tile(dim, pref):
    t = min(dim, pref)
    while dim % t:
        t -= 1
    return t


def _params(*sem):
    return pltpu.CompilerParams(dimension_semantics=sem, vmem_limit_bytes=V7X_VMEM_LIMIT_BYTES)


def _dot(a, b):
    return jnp.dot(a, b, preferred_element_type=F32)


def _dot_nt(a, b):
    return lax.dot_general(a, b, (((1,), (1,)), ((), ())), preferred_element_type=F32)


def _rmsnorm_cast_kernel(x_ref, g_ref, o_ref, *, eps):
    x = x_ref[...]
    ms = jnp.mean(x * x, axis=-1, keepdims=True)
    o_ref[...] = (x * lax.rsqrt(ms + eps) * g_ref[...]).astype(o_ref.dtype)


def _rmsnorm_cast(x, g, eps):
    m, d = x.shape
    tm = _tile(m, 256)
    return pl.pallas_call(
        functools.partial(_rmsnorm_cast_kernel, eps=eps),
        grid=(m // tm,),
        in_specs=[pl.BlockSpec((tm, d), lambda i: (i, 0)), pl.BlockSpec((1, d), lambda i: (0, 0))],
        out_specs=pl.BlockSpec((tm, d), lambda i: (i, 0)),
        out_shape=jax.ShapeDtypeStruct((m, d), BF16),
        compiler_params=_params("parallel"),
        name="rmsnorm_cast",
    )(x, g.reshape(1, d).astype(F32))


def _with_casts(body, n_in, n_out, n_cast):
    def wrapped(*refs):
        main_in, cast_in = refs[:n_in], refs[n_in:n_in + n_cast]
        outs = refs[n_in + n_cast:]
        body(*main_in, *outs[:n_out])
        for src, dst in zip(cast_in, outs[n_out:]):
            dst[...] = src[...].astype(dst.dtype)
    return wrapped


def _tiled_call(body, rows, weights, tiles, vecs, out_dtypes, n_cols, tm, tn, name, casts=()):
    m = rows[0].shape[0]
    tm = _tile(m, tm)
    tn = _tile(n_cols, tn)
    steps = (m // tm) * (n_cols // tn)
    nj = n_cols // tn
    hosted = [c for c in casts if c[0].shape[0] % (steps * BF16_SUBLANES) == 0 and c[1] % c[2] == 0]
    if len(hosted) != len(casts):
        res = _tiled_call(body, rows, weights, tiles, vecs, out_dtypes, n_cols, tm, tn, name)
        return list(res) + [c[0][:, c[1]:c[1] + c[2]].astype(BF16) for c in casts]
    in_specs, args = [], []
    for a in rows:
        in_specs.append(pl.BlockSpec((tm, a.shape[1]), lambda i, j: (i, 0)))
        args.append(a)
    for w, c0 in weights:
        assert c0 % tn == 0
        in_specs.append(pl.BlockSpec((w.shape[0], tn), lambda i, j, o=c0 // tn: (0, j + o)))
        args.append(w)
    for t, c0 in tiles:
        assert c0 % tn == 0
        in_specs.append(pl.BlockSpec((tm, tn), lambda i, j, o=c0 // tn: (i, j + o)))
        args.append(t)
    for v in vecs:
        if isinstance(v, tuple):
            in_specs.append(pl.BlockSpec((1, tn), lambda i, j: (0, j)))
            args.append(v[1])
        else:
            in_specs.append(pl.BlockSpec(v.shape, lambda i, j: (0, 0)))
            args.append(v)
    n_in = len(args)
    out_specs, out_shape, j_sem = [], [], "parallel"
    for dt in out_dtypes:
        if not isinstance(dt, tuple):
            out_specs.append(pl.BlockSpec((tm, tn), lambda i, j: (i, j)))
            out_shape.append(jax.ShapeDtypeStruct((m, n_cols), dt))
        elif dt[0] == "rows":
            groups = n_cols // LANES
            out_specs.append(pl.BlockSpec((tm * groups, LANES), lambda i, j: (i, 0)))
            out_shape.append(jax.ShapeDtypeStruct((m * groups, LANES), dt[1]))
            j_sem = "arbitrary"
        else:
            out_specs.append(pl.BlockSpec((tm, LANES), lambda i, j: (i, j)))
            out_shape.append(jax.ShapeDtypeStruct((m, nj * LANES), dt[1]))
    for src, c0, n in casts:
        slab = src.shape[0] // steps
        in_specs.append(pl.BlockSpec((slab, n), lambda i, j, o=c0 // n: (i * nj + j, o)))
        args.append(src)
        out_specs.append(pl.BlockSpec((slab, n), lambda i, j: (i * nj + j, 0)))
        out_shape.append(jax.ShapeDtypeStruct((src.shape[0], n), BF16))
    if casts:
        body = _with_casts(body, n_in, len(out_dtypes), len(casts))
    return pl.pallas_call(
        body,
        grid=(m // tm, n_cols // tn),
        in_specs=in_specs,
        out_specs=out_specs,
        out_shape=out_shape,
        compiler_params=_params("parallel", j_sem),
        name=name,
    )(*args)


def _head_rmsnorm_store(acc, g, eps, scale, out_refs):
    group = g.shape[1]
    for c in range(acc.shape[1] // group):
        sl = slice(c * group, (c + 1) * group)
        blk = acc[:, sl]
        ms = jnp.mean(blk * blk, axis=-1, keepdims=True)
        y = blk * lax.rsqrt(ms + eps) * g
        for ref, s in zip(out_refs, scale):
            ref[:, sl] = (y if s == 1.0 else y * s).astype(ref.dtype)


def _q_body(a_ref, w_ref, g_ref, q_ref, *, scale):
    _head_rmsnorm_store(_dot(a_ref[...], w_ref[...]), g_ref[...], NORM_EPS, (scale,), (q_ref,))


def _k_body(a_ref, w_ref, g_ref, kf_ref, kb_ref, *, groups):
    acc = _dot(a_ref[...], w_ref[...])
    g = g_ref[...]
    tm, tn = acc.shape
    first = pl.program_id(1) * (tn // LANES)
    for c in range(tn // LANES):
        blk = acc[:, c * LANES:(c + 1) * LANES]
        ms = jnp.mean(blk * blk, axis=-1, keepdims=True)
        y = blk * lax.rsqrt(ms + NORM_EPS) * g
        kf_ref[pl.ds(first + c, tm, stride=groups), :] = y
        kb_ref[:, c * LANES:(c + 1) * LANES] = y.astype(kb_ref.dtype)


def _v_body(a_ref, w_ref, vf_ref, vb_ref):
    acc = _dot(a_ref[...], w_ref[...])
    vf_ref[...] = acc
    vb_ref[...] = acc.astype(vb_ref.dtype)


def _u_body(a_ref, w_ref, u_ref):
    u_ref[...] = _dot(a_ref[...], w_ref[...])


def _gate_body(a_ref, w_ref, g_ref):
    g_ref[...] = jax.nn.sigmoid(_dot(a_ref[...], w_ref[...])).astype(g_ref.dtype)


def _merge_body(o_ref, yb_ref, wa_ref, wp_ref, ga_ref, gb_ref, out_ref):
    ya = _dot(o_ref[...], wa_ref[...])
    yb = _dot(yb_ref[...], wp_ref[...])
    out_ref[...] = (ga_ref[...].astype(F32) * ya + gb_ref[...].astype(F32) * yb).astype(out_ref.dtype)


def _residual_norm_body(a_ref, w_ref, x_ref, g_ref, out_ref, xg_ref, ssq_ref):
    x1 = x_ref[...] + _dot(a_ref[...], w_ref[...])
    out_ref[...] = x1
    xg_ref[...] = (x1 * g_ref[...]).astype(xg_ref.dtype)
    ssq_ref[...] = jnp.broadcast_to(jnp.sum(x1 * x1, axis=-1, keepdims=True), ssq_ref.shape)


def _relu2_norm_body(a_ref, ssq_ref, w_ref, out_ref, *, dim):
    ssq = ssq_ref[...]
    tot = ssq[:, :LANES]
    for c in range(1, ssq.shape[1] // LANES):
        tot = tot + ssq[:, c * LANES:(c + 1) * LANES]
    r = lax.rsqrt(tot * (1.0 / dim) + NORM_EPS)
    acc = _dot(a_ref[...], w_ref[...])
    h = jnp.maximum(acc * _rep(r, acc.shape[1]), 0.0)
    out_ref[...] = (h * h).astype(out_ref.dtype)


def _down_body(a_ref, w_ref, x_ref, out_ref):
    @pl.when(pl.program_id(2) == 0)
    def _():
        out_ref[...] = x_ref[...]

    out_ref[...] += _dot(a_ref[...], w_ref[...])


def _down_proj(a, w, x, tm, tn, tk):
    m, kdim = a.shape
    n = w.shape[1]
    tm, tn, tk = _tile(m, tm), _tile(n, tn), _tile(kdim, tk)
    return pl.pallas_call(
        _down_body,
        grid=(m // tm, n // tn, kdim // tk),
        in_specs=[
            pl.BlockSpec((tm, tk), lambda i, j, k: (i, k)),
            pl.BlockSpec((tk, tn), lambda i, j, k: (k, j)),
            pl.BlockSpec((tm, tn), lambda i, j, k: (i, j)),
        ],
        out_specs=pl.BlockSpec((tm, tn), lambda i, j, k: (i, j)),
        out_shape=jax.ShapeDtypeStruct((m, n), F32),
        compiler_params=_params("parallel", "parallel", "arbitrary"),
        name="down_proj",
    )(a, w, x)


def _rep(x, width):
    if width <= STAT_LANES:
        return x[:, :width]
    return jnp.concatenate([x] * (width // STAT_LANES), axis=1)


def _lambda(lq1, lk1, lq2, lk2, lam_init):
    return (jnp.exp(jnp.sum(lq1[...] * lk1[...], keepdims=True))
            - jnp.exp(jnp.sum(lq2[...] * lk2[...], keepdims=True)) + lam_init)


def _subln(o, gs, lam_init):
    ms = jnp.mean(o * o, axis=-1, keepdims=True)
    return o * lax.rsqrt(ms + SUBLN_EPS) * gs * (1.0 - lam_init)


def _attn_prompt_kernel(q_ref, k_ref, v_ref, lq1, lk1, lq2, lk2, gs_ref, o_ref, *, tq, d, lam_init):
    t = q_ref.shape[0]
    lam = _lambda(lq1, lk1, lq2, lk2, lam_init)
    gs = gs_ref[...]
    rq = lax.broadcasted_iota(jnp.int32, (tq, tq), 0) // CHUNK
    rk = lax.broadcasted_iota(jnp.int32, (tq, tq), 1) // CHUNK
    diag_mask = rk <= rq
    for i in range(t // tq):
        r0 = i * tq
        w_diag, w_prev = None, None
        for c in range(2):
            qc = q_ref[r0:r0 + tq, c * d:(c + 1) * d]
            s_diag = jnp.where(diag_mask, _dot_nt(qc, k_ref[r0:r0 + tq, c * d:(c + 1) * d]), NEG_INF)
            m = jnp.max(s_diag, axis=-1, keepdims=True)
            if i > 0:
                s_prev = _dot_nt(qc, k_ref[0:r0, c * d:(c + 1) * d])
                m = jnp.maximum(m, jnp.max(s_prev, axis=-1, keepdims=True))
            p_diag = jnp.exp2(s_diag - m)
            l = jnp.sum(p_diag, axis=-1, keepdims=True)
            if i > 0:
                p_prev = jnp.exp2(s_prev - m)
                l = l + jnp.sum(p_prev, axis=-1, keepdims=True)
            if c == 0:
                l1, w_diag = l, p_diag
                if i > 0:
                    w_prev = p_prev
            else:
                ratio = -lam * l1 / l
                w_diag = w_diag + p_diag * ratio
                if i > 0:
                    w_prev = w_prev + p_prev * ratio
        o = _dot(w_diag.astype(BF16), v_ref[r0:r0 + tq, :])
        if i > 0:
            o = o + _dot(w_prev.astype(BF16), v_ref[0:r0, :])
        o_ref[r0:r0 + tq, :] = _subln(o / l1, gs, lam_init).astype(o_ref.dtype)


def _attn_prompt(q, k, v, lam_vecs, g_subln, n_heads, lam_init):
    b, t, width = q.shape
    hw = width // n_heads
    d = hw // 2
    tq = _tile(t, 256)
    assert tq % CHUNK == 0
    vec = pl.BlockSpec((1, d), lambda bi, h: (0, 0))
    seq = pl.BlockSpec((None, t, hw), lambda bi, h: (bi, 0, h))
    return pl.pallas_call(
        functools.partial(_attn_prompt_kernel, tq=tq, d=d, lam_init=lam_init),
        grid=(b, n_heads),
        in_specs=[seq, seq, seq, vec, vec, vec, vec, pl.BlockSpec((1, hw), lambda bi, h: (0, 0))],
        out_specs=seq,
        out_shape=jax.ShapeDtypeStruct((b, t, width), BF16),
        compiler_params=_params("parallel", "parallel"),
        name="attn_prompt",
    )(q, k, v, *lam_vecs, g_subln)


def _attn_sample_kernel(q_ref, ck_ref, cv0_ref, cv1_ref, kn_ref, vn_ref, lq1, lk1, lq2, lk2, gs_ref, o_ref,
                        s_sc, p_sc, m_sc, l_sc, acc_sc, *, n_heads, d, tk, past, lam_init):
    kb = pl.program_id(1)
    nkb = pl.num_programs(1) - 1
    hw = 2 * d
    tq = q_ref.shape[0]

    @pl.when(kb == 0)
    def _():
        m_sc[...] = jnp.full(m_sc.shape, -jnp.inf, F32)
        l_sc[...] = jnp.zeros(l_sc.shape, F32)
        acc_sc[...] = jnp.zeros(acc_sc.shape, F32)

    def update(k_of, v_of, mask, width):
        for h in range(n_heads):
            for c in range(2):
                s = _dot_nt(q_ref[:, h * hw + c * d:h * hw + (c + 1) * d], k_of(h, c))
                if mask is not None:
                    s = jnp.where(mask, s, NEG_INF)
                s_sc[(2 * h + c) * tq:(2 * h + c + 1) * tq, :width] = s
        s = s_sc[:, :width]
        m_old = m_sc[...]
        m_new = jnp.maximum(m_old, jnp.max(s, axis=-1, keepdims=True))
        alpha = jnp.exp2(m_old - m_new)
        p = jnp.exp2(s - _rep(m_new, width))
        l_sc[...] = alpha * l_sc[...] + jnp.sum(p, axis=-1, keepdims=True)
        m_sc[...] = m_new
        p_sc[:, :width] = p.astype(p_sc.dtype)
        for h in range(n_heads):
            rows = slice(2 * h * tq, (2 * h + 2) * tq)
            pv = _dot(p_sc[rows, :width], v_of(h))
            acc_sc[rows, :] = _rep(alpha[rows], hw) * acc_sc[rows, :] + pv

    @pl.when(kb == 0)
    def _():
        rq = (past + lax.broadcasted_iota(jnp.int32, (tq, tq), 0)) // CHUNK
        rk = (past + lax.broadcasted_iota(jnp.int32, (tq, tq), 1)) // CHUNK
        update(lambda h, c: kn_ref[:, h * hw + c * d:h * hw + (c + 1) * d],
               lambda h: vn_ref[:, h * hw:(h + 1) * hw], rk <= rq, tq)

    @pl.when(kb > 0)
    def _():
        update(lambda h, c: ck_ref[pl.ds(2 * h + c, tk, stride=2 * n_heads), :].astype(BF16),
               lambda h: jnp.concatenate([cv0_ref[pl.ds(h, tk, stride=n_heads), :],
                                          cv1_ref[pl.ds(h, tk, stride=n_heads), :]], axis=1).astype(BF16),
               None, tk)

    @pl.when(kb == nkb)
    def _():
        lam = _lambda(lq1, lk1, lq2, lk2, lam_init)
        for h in range(n_heads):
            r1, r2 = slice(2 * h * tq, (2 * h + 1) * tq), slice((2 * h + 1) * tq, (2 * h + 2) * tq)
            o = acc_sc[r1, :] / _rep(l_sc[r1, :], hw) - lam * (acc_sc[r2, :] / _rep(l_sc[r2, :], hw))
            o_ref[:, h * hw:(h + 1) * hw] = _subln(o, gs_ref[...], lam_init).astype(o_ref.dtype)


def _attn_sample(q, cache_k, cache_v, k_new, v_new, lam_vecs, g_subln, n_heads, lam_init):
    b, tq, width = q.shape
    hw = width // n_heads
    d = hw // 2
    assert d == STAT_LANES
    past = cache_v.shape[1] // n_heads
    tk = _tile(past, 512)
    nkb = past // tk
    vec = pl.BlockSpec((1, d), lambda bi, kb: (0, 0))
    new = pl.BlockSpec((None, tq, width), lambda bi, kb: (bi, 0, 0))
    ck_spec = pl.BlockSpec((None, tk * 2 * n_heads, d), lambda bi, kb: (bi, jnp.maximum(kb - 1, 0), 0))
    cv_specs = [pl.BlockSpec((None, tk * n_heads, d), lambda bi, kb, c=c: (bi, jnp.maximum(kb - 1, 0), c))
                for c in range(2)]
    return pl.pallas_call(
        functools.partial(_attn_sample_kernel, n_heads=n_heads, d=d, tk=tk, past=past, lam_init=lam_init),
        grid=(b, nkb + 1),
        in_specs=[new, ck_spec, *cv_specs, new, new, vec, vec, vec, vec,
                  pl.BlockSpec((1, hw), lambda bi, kb: (0, 0))],
        out_specs=new,
        out_shape=jax.ShapeDtypeStruct((b, tq, width), BF16),
        scratch_shapes=[
            pltpu.VMEM((2 * n_heads * tq, tk), F32),
            pltpu.VMEM((2 * n_heads * tq, tk), BF16),
            pltpu.VMEM((2 * n_heads * tq, STAT_LANES), F32),
            pltpu.VMEM((2 * n_heads * tq, STAT_LANES), F32),
            pltpu.VMEM((2 * n_heads * tq, hw), F32),
        ],
        compiler_params=_params("parallel", "arbitrary"),
        name="attn_sample",
    )(q, cache_k, cache_v, cache_v, k_new, v_new, *lam_vecs, g_subln)


def _pool_mix_kernel(u_ref, prev_ref, hist_ref, wp_ref, sc_ref, o_ref, ext_ref, *, tt, pos0):
    t = pl.program_id(1)
    ext_ref[0:HIST_ROWS, :] = jnp.where(t == 0, hist_ref[...], prev_ref[...])
    ext_ref[HIST_ROWS:HIST_ROWS + tt, :] = u_ref[...]
    pos = pos0 + t * tt + lax.broadcasted_iota(jnp.int32, (tt, 1), 0)
    gw = u_ref.shape[1] // len(POOL_WINDOWS)
    for g, w in enumerate(POOL_WINDOWS):
        cols = slice(g * gw, (g + 1) * gw)
        win = ext_ref[HIST_ROWS:HIST_ROWS + tt, cols]
        for back in range(1, w):
            win = win + ext_ref[HIST_ROWS - back:HIST_ROWS - back + tt, cols]
        count = jnp.minimum(pos + 1, w).astype(F32)
        pooled = win / count - u_ref[:, cols]
        mixed = _dot(pooled.astype(BF16), wp_ref[g]) * sc_ref[:, cols]
        o_ref[:, cols] = mixed.astype(o_ref.dtype)


def _pool_mix(u, hist, w_pool, pool_scale, pos0):
    b, t, width = u.shape
    tt = _tile(t, 256)
    assert tt % HIST_ROWS == 0
    hist16 = jnp.pad(hist, ((0, 0), (HIST_ROWS - POOL_HIST, 0), (0, 0)))
    per = tt // HIST_ROWS
    return pl.pallas_call(
        functools.partial(_pool_mix_kernel, tt=tt, pos0=pos0),
        grid=(b, t // tt),
        in_specs=[
            pl.BlockSpec((None, tt, width), lambda bi, ti: (bi, ti, 0)),
            pl.BlockSpec((None, HIST_ROWS, width), lambda bi, ti: (bi, jnp.maximum(ti * per - 1, 0), 0)),
            pl.BlockSpec((None, HIST_ROWS, width), lambda bi, ti: (bi, 0, 0)),
            pl.BlockSpec(w_pool.shape, lambda bi, ti: (0, 0, 0)),
            pl.BlockSpec((1, width), lambda bi, ti: (0, 0)),
        ],
        out_specs=pl.BlockSpec((None, tt, width), lambda bi, ti: (bi, ti, 0)),
        out_shape=jax.ShapeDtypeStruct((b, t, width), BF16),
        scratch_shapes=[pltpu.VMEM((HIST_ROWS + tt, width), F32)],
        compiler_params=_params("parallel", "parallel"),
        name="pool_mix",
    )(u, u, hist16, w_pool, pool_scale)


def _trunk_layer(x, pos0, pool_hist, kv_cache, lam_init, w, wb):
    b, t, dm = x.shape
    m = b * t
    n_heads, d = w["n_heads"], w["head_dim"]
    qk_w = n_heads * 2 * d
    attn_w, pool_w = qk_w, w["w_pool_out"].shape[0]
    c_k, c_v, c_u, c_g = qk_w, 2 * qk_w, 2 * qk_w + attn_w, 2 * qk_w + attn_w + pool_w
    x2 = x.reshape(m, dm)
    w_in = w["w_in"]
    host = wb is None
    if host:
        wb = {"q": w_in[:, :qk_w].astype(BF16), "pool": w["w_pool"].astype(BF16)}

    def call(names, casts, *args):
        res = _tiled_call(*args, casts=casts if host else ())
        n_main = len(res) - (len(names) if host else 0)
        if host:
            wb.update(zip(names, res[n_main:]))
        return res[:n_main]

    xn = _rmsnorm_cast(x2, w["g_norm1"], NORM_EPS)
    (q,) = call(("k", "v", "u"), [(w_in, c_k, qk_w), (w_in, c_v, attn_w), (w_in, c_u, pool_w)],
                functools.partial(_q_body, scale=d ** -0.5 * LOG2E), [xn], [(wb["q"], 0)], [], [w["g_q"]], [BF16], qk_w,
                1024, 1024, "proj_q")
    k_f, k_b = call(("attn_out", "g"), [(w["w_attn_out"], 0, dm), (w_in, c_g, 2 * dm)],
                    functools.partial(_k_body, groups=qk_w // LANES), [xn], [(wb["k"], 0)], [], [w["g_k"]],
                    [("rows", F32), BF16], qk_w, 512, 1024, "proj_k")
    v_f, v_b = call(("pool_out",), [(w["w_pool_out"], 0, dm)],
                    _v_body, [xn], [(wb["v"], 0)], [], [], [F32, BF16], attn_w, 1024, 1024, "proj_v")
    (u,) = _tiled_call(_u_body, [xn], [(wb["u"], 0)], [], [], [F32], pool_w, 1024, 1024, "proj_u")
    (gates,) = call(("up", "o"), [(w["w_up"], 0, w["w_up"].shape[1]), (w["w_o"], 0, dm)],
                    _gate_body, [xn], [(wb["g"], 0)], [], [], [BF16], 2 * dm, 1024, 1024, "proj_gates")

    lam_vecs = w["lam_vecs"]
    if kv_cache is None:
        o = _attn_prompt(q.reshape(b, t, qk_w), k_b.reshape(b, t, qk_w), v_b.reshape(b, t, attn_w),
                         lam_vecs, w["g_subln"], n_heads, lam_init)
    else:
        ck, cv = kv_cache
        o = _attn_sample(q.reshape(b, t, qk_w), ck, cv, k_b.reshape(b, t, qk_w), v_b.reshape(b, t, attn_w),
                         lam_vecs, w["g_subln"], n_heads, lam_init)

    u3 = u.reshape(b, t, pool_w)
    yb_in = _pool_mix(u3, pool_hist, wb["pool"], w["pool_scale"], pos0)
    assert t >= POOL_HIST
    pool_state = u3[:, t - POOL_HIST:, :]

    (merged,) = _tiled_call(_merge_body, [o.reshape(m, attn_w), yb_in.reshape(m, pool_w)],
                            [(wb["attn_out"], 0), (wb["pool_out"], 0)], [(gates, 0), (gates, dm)], [],
                            [BF16], dm, 1024, 1024, "merge")
    x1, x1g, ssq = _tiled_call(_residual_norm_body, [merged], [(wb["o"], 0)], [(x2, 0)],
                               [("cols", w["g_norm2"])], [F32, BF16, ("stat", F32)], dm, 1024, 512, "out_proj")
    (act,) = call(("down",), [(w["w_down"], 0, dm)],
                  functools.partial(_relu2_norm_body, dim=dm), [x1g, ssq], [(wb["up"], 0)], [], [], [BF16],
                  w["w_up"].shape[1], 1024, 1024, "mlp_up")
    y = _down_proj(act, wb["down"], x1, 1024, 1024, 4096)
    return (y.reshape(b, t, dm), k_f.reshape(b, t, n_heads, 2, d), v_f.reshape(b, t, n_heads, 2 * d),
            pool_state), wb


def kernel(x_prompt, x_sample, cache_k, cache_v, state_pool, g_norm1, w_in, g_q, g_k, lambda_q1, lambda_k1,
           lambda_q2, lambda_k2, g_subln, w_attn_out, w_pool, pool_scale, w_pool_out, w_o, g_norm2, w_up,
           w_down):
    depth, dec_b, past, n_heads, _, d = cache_k.shape
    y_p, y_s = x_prompt, x_sample
    hist_p = jnp.zeros((x_prompt.shape[0], POOL_HIST, w_pool_out.shape[1]), F32)
    outs = [[] for _ in range(6)]
    for l in range(depth):
        lam_init = 0.8 - 0.6 * math.exp(-0.3 * l)
        row = lambda a: a[l].reshape(1, -1).astype(F32)
        w = dict(
            n_heads=n_heads, head_dim=d,
            g_norm1=g_norm1[l], g_norm2=row(g_norm2), g_q=row(g_q), g_k=row(g_k), g_subln=row(g_subln),
            lam_vecs=(row(lambda_q1), row(lambda_k1), row(lambda_q2), row(lambda_k2)),
            pool_scale=row(pool_scale),
            w_in=w_in[l], w_attn_out=w_attn_out[l], w_pool=w_pool[l], w_pool_out=w_pool_out[l], w_o=w_o[l],
            w_up=w_up[l], w_down=w_down[l],
        )
        (y_p, kp, vp, sp), wb = _trunk_layer(y_p, 0, hist_p, None, lam_init, w, None)
        kv = (cache_k[l].reshape(dec_b, past * n_heads * 2, d), cache_v[l].reshape(dec_b, past * n_heads, 2 * d))
        (y_s, ks, vs, ss), _ = _trunk_layer(y_s, past, state_pool[l], kv, lam_init, w, wb)
        for lst, val in zip(outs, (kp, vp, sp, ks, vs, ss)):
            lst.append(val)
    return (y_p, y_s) + tuple(jnp.stack(o) for o in outs)
```

```python
import functools
import math

import jax
import jax.numpy as jnp
from jax import lax
from jax.experimental import pallas as pl
from jax.experimental.pallas import tpu as pltpu

CHUNK = 64
POOL_WINDOWS = (2, 4, 8, 16)
POOL_HIST = max(POOL_WINDOWS) - 1
HIST_ROWS = 16
NORM_EPS = 1e-6
SUBLN_EPS = 1e-5
NEG_INF = -1e30
LOG2E = math.log2(math.e)
V7X_VMEM_LIMIT_BYTES = 56 * 1024 * 1024
LANES = 128
STAT_LANES = LANES
BF16_SUBLANES = 16

F32 = jnp.float32
BF16 = jnp.bfloat16


def _tile(dim, pref):
    t = min(dim, pref)
    while dim % t:
        t -= 1
    return t


def _params(*sem):
    return pltpu.CompilerParams(dimension_semantics=sem, vmem_limit_bytes=V7X_VMEM_LIMIT_BYTES)


def _dot(a, b):
    return jnp.dot(a, b, preferred_element_type=F32)


def _dot_nt(a, b):
    return lax.dot_general(a, b, (((1,), (1,)), ((), ())), preferred_element_type=F32)


def _rmsnorm_cast_kernel(x_ref, g_ref, o_ref, *, eps):
    x = x_ref[...]
    ms = jnp.mean(x * x, axis=-1, keepdims=True)
    o_ref[...] = (x * lax.rsqrt(ms + eps) * g_ref[...]).astype(o_ref.dtype)


def _rmsnorm_cast(x, g, eps):
    m, d = x.shape
    tm = _tile(m, 256)
    return pl.pallas_call(
        functools.partial(_rmsnorm_cast_kernel, eps=eps),
        grid=(m // tm,),
        in_specs=[pl.BlockSpec((tm, d), lambda i: (i, 0)), pl.BlockSpec((1, d), lambda i: (0, 0))],
        out_specs=pl.BlockSpec((tm, d), lambda i: (i, 0)),
        out_shape=jax.ShapeDtypeStruct((m, d), BF16),
        compiler_params=_params("parallel"),
        name="rmsnorm_cast",
    )(x, g.reshape(1, d).astype(F32))


def _with_casts(body, n_in, n_out, n_cast):
    def wrapped(*refs):
        main_in, cast_in = refs[:n_in], refs[n_in:n_in + n_cast]
        outs = refs[n_in + n_cast:]
        body(*main_in, *outs[:n_out])
        for src, dst in zip(cast_in, outs[n_out:]):
            dst[...] = src[...].astype(dst.dtype)
    return wrapped


def _tiled_call(body, rows, weights, tiles, vecs, out_dtypes, n_cols, tm, tn, name, casts=()):
    m = rows[0].shape[0]
    tm = _tile(m, tm)
    tn = _tile(n_cols, tn)
    steps = (m // tm) * (n_cols // tn)
    nj = n_cols // tn
    hosted = [c for c in casts if c[0].shape[0] % (steps * BF16_SUBLANES) == 0 and c[1] % c[2] == 0]
    if len(hosted) != len(casts):
        res = _tiled_call(body, rows, weights, tiles, vecs, out_dtypes, n_cols, tm, tn, name)
        return list(res) + [c[0][:, c[1]:c[1] + c[2]].astype(BF16) for c in casts]
    in_specs, args = [], []
    for a in rows:
        in_specs.append(pl.BlockSpec((tm, a.shape[1]), lambda i, j: (i, 0)))
        args.append(a)
    for w, c0 in weights:
        assert c0 % tn == 0
        in_specs.append(pl.BlockSpec((w.shape[0], tn), lambda i, j, o=c0 // tn: (0, j + o)))
        args.append(w)
    for t, c0 in tiles:
        assert c0 % tn == 0
        in_specs.append(pl.BlockSpec((tm, tn), lambda i, j, o=c0 // tn: (i, j + o)))
        args.append(t)
    for v in vecs:
        if isinstance(v, tuple):
            in_specs.append(pl.BlockSpec((1, tn), lambda i, j: (0, j)))
            args.append(v[1])
        else:
            in_specs.append(pl.BlockSpec(v.shape, lambda i, j: (0, 0)))
            args.append(v)
    n_in = len(args)
    out_specs, out_shape, j_sem = [], [], "parallel"
    for dt in out_dtypes:
        if not isinstance(dt, tuple):
            out_specs.append(pl.BlockSpec((tm, tn), lambda i, j: (i, j)))
            out_shape.append(jax.ShapeDtypeStruct((m, n_cols), dt))
        elif dt[0] == "rows":
            groups = n_cols // LANES
            out_specs.append(pl.BlockSpec((tm * groups, LANES), lambda i, j: (i, 0)))
            out_shape.append(jax.ShapeDtypeStruct((m * groups, LANES), dt[1]))
            j_sem = "arbitrary"
        else:
            out_specs.append(pl.BlockSpec((tm, LANES), lambda i, j: (i, j)))
            out_shape.append(jax.ShapeDtypeStruct((m, nj * LANES), dt[1]))
    for src, c0, n in casts:
        slab = src.shape[0] // steps
        in_specs.append(pl.BlockSpec((slab, n), lambda i, j, o=c0 // n: (i * nj + j, o)))
        args.append(src)
        out_specs.append(pl.BlockSpec((slab, n), lambda i, j: (i * nj + j, 0)))
        out_shape.append(jax.ShapeDtypeStruct((src.shape[0], n), BF16))
    if casts:
        body = _with_casts(body, n_in, len(out_dtypes), len(casts))
    return pl.pallas_call(
        body,
        grid=(m // tm, n_cols // tn),
        in_specs=in_specs,
        out_specs=out_specs,
        out_shape=out_shape,
        compiler_params=_params("parallel", j_sem),
        name=name,
    )(*args)


def _head_rmsnorm_store(acc, g, eps, scale, out_refs):
    group = g.shape[1]
    for c in range(acc.shape[1] // group):
        sl = slice(c * group, (c + 1) * group)
        blk = acc[:, sl]
        ms = jnp.mean(blk * blk, axis=-1, keepdims=True)
        y = blk * lax.rsqrt(ms + eps) * g
        for ref, s in zip(out_refs, scale):
            ref[:, sl] = (y if s == 1.0 else y * s).astype(ref.dtype)


def _q_body(a_ref, w_ref, g_ref, q_ref, *, scale):
    _head_rmsnorm_store(_dot(a_ref[...], w_ref[...]), g_ref[...], NORM_EPS, (scale,), (q_ref,))


def _k_body(a_ref, w_ref, g_ref, kf_ref, kb_ref, *, groups):
    acc = _dot(a_ref[...], w_ref[...])
    g = g_ref[...]
    tm, tn = acc.shape
    first = pl.program_id(1) * (tn // LANES)
    for c in range(tn // LANES):
        blk = acc[:, c * LANES:(c + 1) * LANES]
        ms = jnp.mean(blk * blk, axis=-1, keepdims=True)
        y = blk * lax.rsqrt(ms + NORM_EPS) * g
        kf_ref[pl.ds(first + c, tm, stride=groups), :] = y
        kb_ref[:, c * LANES:(c + 1) * LANES] = y.astype(kb_ref.dtype)


def _v_body(a_ref, w_ref, vf_ref, vb_ref):
    acc = _dot(a_ref[...], w_ref[...])
    vf_ref[...] = acc
    vb_ref[...] = acc.astype(vb_ref.dtype)


def _u_body(a_ref, w_ref, u_ref):
    u_ref[...] = _dot(a_ref[...], w_ref[...])


def _gate_body(a_ref, w_ref, g_ref):
    g_ref[...] = jax.nn.sigmoid(_dot(a_ref[...], w_ref[...])).astype(g_ref.dtype)


def _merge_body(o_ref, yb_ref, wa_ref, wp_ref, ga_ref, gb_ref, out_ref):
    ya = _dot(o_ref[...], wa_ref[...])
    yb = _dot(yb_ref[...], wp_ref[...])
    out_ref[...] = (ga_ref[...].astype(F32) * ya + gb_ref[...].astype(F32) * yb).astype(out_ref.dtype)


def _residual_norm_body(a_ref, w_ref, x_ref, g_ref, out_ref, xg_ref, ssq_ref):
    x1 = x_ref[...] + _dot(a_ref[...], w_ref[...])
    out_ref[...] = x1
    xg_ref[...] = (x1 * g_ref[...]).astype(xg_ref.dtype)
    ssq_ref[...] = jnp.broadcast_to(jnp.sum(x1 * x1, axis=-1, keepdims=True), ssq_ref.shape)


def _relu2_norm_body(a_ref, ssq_ref, w_ref, out_ref, *, dim):
    ssq = ssq_ref[...]
    tot = ssq[:, :LANES]
    for c in range(1, ssq.shape[1] // LANES):
        tot = tot + ssq[:, c * LANES:(c + 1) * LANES]
    r = lax.rsqrt(tot * (1.0 / dim) + NORM_EPS)
    acc = _dot(a_ref[...], w_ref[...])
    h = jnp.maximum(acc * _rep(r, acc.shape[1]), 0.0)
    out_ref[...] = (h * h).astype(out_ref.dtype)


def _down_body(a_ref, w_ref, x_ref, out_ref):
    @pl.when(pl.program_id(2) == 0)
    def _():
        out_ref[...] = x_ref[...]

    out_ref[...] += _dot(a_ref[...], w_ref[...])


def _down_proj(a, w, x, tm, tn, tk):
    m, kdim = a.shape
    n = w.shape[1]
    tm, tn, tk = _tile(m, tm), _tile(n, tn), _tile(kdim, tk)
    return pl.pallas_call(
        _down_body,
        grid=(m // tm, n // tn, kdim // tk),
        in_specs=[
            pl.BlockSpec((tm, tk), lambda i, j, k: (i, k)),
            pl.BlockSpec((tk, tn), lambda i, j, k: (k, j)),
            pl.BlockSpec((tm, tn), lambda i, j, k: (i, j)),
        ],
        out_specs=pl.BlockSpec((tm, tn), lambda i, j, k: (i, j)),
        out_shape=jax.ShapeDtypeStruct((m, n), F32),
        compiler_params=_params("parallel", "parallel", "arbitrary"),
        name="down_proj",
    )(a, w, x)


def _rep(x, width):
    if width <= STAT_LANES:
        return x[:, :width]
    return jnp.concatenate([x] * (width // STAT_LANES), axis=1)


def _lambda(lq1, lk1, lq2, lk2, lam_init):
    return (jnp.exp(jnp.sum(lq1[...] * lk1[...], keepdims=True))
            - jnp.exp(jnp.sum(lq2[...] * lk2[...], keepdims=True)) + lam_init)


def _subln(o, gs, lam_init):
    ms = jnp.mean(o * o, axis=-1, keepdims=True)
    return o * lax.rsqrt(ms + SUBLN_EPS) * gs * (1.0 - lam_init)


def _attn_prompt_kernel(q_ref, k_ref, v_ref, lq1, lk1, lq2, lk2, gs_ref, o_ref, *, tq, d, lam_init):
    t = q_ref.shape[0]
    lam = _lambda(lq1, lk1, lq2, lk2, lam_init)
    gs = gs_ref[...]
    rq = lax.broadcasted_iota(jnp.int32, (tq, tq), 0) // CHUNK
    rk = lax.broadcasted_iota(jnp.int32, (tq, tq), 1) // CHUNK
    diag_mask = rk <= rq
    for i in range(t // tq):
        r0 = i * tq
        w_diag, w_prev = None, None
        for c in range(2):
            qc = q_ref[r0:r0 + tq, c * d:(c + 1) * d]
            s_diag = jnp.where(diag_mask, _dot_nt(qc, k_ref[r0:r0 + tq, c * d:(c + 1) * d]), NEG_INF)
            m = jnp.max(s_diag, axis=-1, keepdims=True)
            if i > 0:
                s_prev = _dot_nt(qc, k_ref[0:r0, c * d:(c + 1) * d])
                m = jnp.maximum(m, jnp.max(s_prev, axis=-1, keepdims=True))
            p_diag = jnp.exp2(s_diag - m)
            l = jnp.sum(p_diag, axis=-1, keepdims=True)
            if i > 0:
                p_prev = jnp.exp2(s_prev - m)
                l = l + jnp.sum(p_prev, axis=-1, keepdims=True)
            if c == 0:
                l1, w_diag = l, p_diag
                if i > 0:
                    w_prev = p_prev
            else:
                ratio = -lam * l1 / l
                w_diag = w_diag + p_diag * ratio
                if i > 0:
                    w_prev = w_prev + p_prev * ratio
        o = _dot(w_diag.astype(BF16), v_ref[r0:r0 + tq, :])
        if i > 0:
            o = o + _dot(w_prev.astype(BF16), v_ref[0:r0, :])
        o_ref[r0:r0 + tq, :] = _subln(o / l1, gs, lam_init).astype(o_ref.dtype)


def _attn_prompt(q, k, v, lam_vecs, g_subln, n_heads, lam_init):
    b, t, width = q.shape
    hw = width // n_heads
    d = hw // 2
    tq = _tile(t, 256)
    assert tq % CHUNK == 0
    vec = pl.BlockSpec((1, d), lambda bi, h: (0, 0))
    seq = pl.BlockSpec((None, t, hw), lambda bi, h: (bi, 0, h))
    return pl.pallas_call(
        functools.partial(_attn_prompt_kernel, tq=tq, d=d, lam_init=lam_init),
        grid=(b, n_heads),
        in_specs=[seq, seq, seq, vec, vec, vec, vec, pl.BlockSpec((1, hw), lambda bi, h: (0, 0))],
        out_specs=seq,
        out_shape=jax.ShapeDtypeStruct((b, t, width), BF16),
        compiler_params=_params("parallel", "parallel"),
        name="attn_prompt",
    )(q, k, v, *lam_vecs, g_subln)


def _attn_sample_kernel(q_ref, ck_ref, cv0_ref, cv1_ref, kn_ref, vn_ref, lq1, lk1, lq2, lk2, gs_ref, o_ref,
                        s_sc, p_sc, m_sc, l_sc, acc_sc, *, n_heads, d, tk, past, lam_init):
    kb = pl.program_id(1)
    nkb = pl.num_programs(1) - 1
    hw = 2 * d
    tq = q_ref.shape[0]
    half = n_heads // 2

    def slot(h, c):
        return (h % half) * 4 + (h // half) * 2 + c

    def rows(first_slot, n_slots=1):
        return slice(first_slot * tq, (first_slot + n_slots) * tq)

    def q_of(h, c):
        return q_ref[:, h * hw + c * d:h * hw + (c + 1) * d]

    @pl.when(kb == 0)
    def _():
        m_sc[...] = jnp.full(m_sc.shape, -jnp.inf, F32)
        l_sc[...] = jnp.zeros(l_sc.shape, F32)
        acc_sc[...] = jnp.zeros(acc_sc.shape, F32)

    def softmax_update(width):
        s = s_sc[:, :width]
        m_old = m_sc[...]
        m_new = jnp.maximum(m_old, jnp.max(s, axis=-1, keepdims=True))
        alpha = jnp.exp2(m_old - m_new)
        p = jnp.exp2(s - _rep(m_new, width))
        l_sc[...] = alpha * l_sc[...] + jnp.sum(p, axis=-1, keepdims=True)
        m_sc[...] = m_new
        p_sc[:, :width] = p.astype(p_sc.dtype)
        return alpha

    @pl.when(kb == 0)
    def _():
        rq = (past + lax.broadcasted_iota(jnp.int32, (tq, tq), 0)) // CHUNK
        rk = (past + lax.broadcasted_iota(jnp.int32, (tq, tq), 1)) // CHUNK
        for h in range(n_heads):
            for c in range(2):
                s = _dot_nt(q_of(h, c), kn_ref[:, h * hw + c * d:h * hw + (c + 1) * d])
                s_sc[rows(slot(h, c)), :tq] = jnp.where(rk <= rq, s, NEG_INF)
        alpha = softmax_update(tq)
        for h in range(n_heads):
            r = rows(slot(h, 0), 2)
            acc_sc[r, :] = _rep(alpha[r], hw) * acc_sc[r, :] + _dot(p_sc[r, :tq], vn_ref[:, h * hw:(h + 1) * hw])

    @pl.when(kb > 0)
    def _():
        odd = lax.broadcasted_iota(jnp.int32, (tq, 2 * tk), 1) % 2 == 1
        for h in range(half):
            for c in range(2):
                k_pair = ck_ref[pl.ds(2 * h + c, 2 * tk, stride=n_heads), :].astype(BF16)
                s = _dot_nt(jnp.concatenate([q_of(h, c), q_of(h + half, c)], axis=0), k_pair)
                s_sc[rows(slot(h, c)), :] = jnp.where(odd, NEG_INF, s[:tq])
                s_sc[rows(slot(h + half, c)), :] = jnp.where(odd, s[tq:], NEG_INF)
        alpha = softmax_update(2 * tk)
        for h in range(half):
            v_pair = jnp.concatenate([cv0_ref[pl.ds(h, 2 * tk, stride=half), :],
                                      cv1_ref[pl.ds(h, 2 * tk, stride=half), :]], axis=1).astype(BF16)
            r = rows(slot(h, 0), 4)
            acc_sc[r, :] = _rep(alpha[r], hw) * acc_sc[r, :] + _dot(p_sc[r, :], v_pair)

    @pl.when(kb == nkb)
    def _():
        lam = _lambda(lq1, lk1, lq2, lk2, lam_init)
        for h in range(n_heads):
            r1, r2 = rows(slot(h, 0)), rows(slot(h, 1))
            o = acc_sc[r1, :] / _rep(l_sc[r1, :], hw) - lam * (acc_sc[r2, :] / _rep(l_sc[r2, :], hw))
            o_ref[:, h * hw:(h + 1) * hw] = _subln(o, gs_ref[...], lam_init).astype(o_ref.dtype)


def _attn_sample(q, cache_k, cache_v, k_new, v_new, lam_vecs, g_subln, n_heads, lam_init):
    b, tq, width = q.shape
    hw = width // n_heads
    d = hw // 2
    assert d == STAT_LANES and n_heads % 2 == 0
    past = cache_v.shape[1] // n_heads
    tk = _tile(past, 512)
    nkb = past // tk
    vec = pl.BlockSpec((1, d), lambda bi, kb: (0, 0))
    new = pl.BlockSpec((None, tq, width), lambda bi, kb: (bi, 0, 0))
    ck_spec = pl.BlockSpec((None, tk * 2 * n_heads, d), lambda bi, kb: (bi, jnp.maximum(kb - 1, 0), 0))
    cv_specs = [pl.BlockSpec((None, tk * n_heads, d), lambda bi, kb, c=c: (bi, jnp.maximum(kb - 1, 0), c))
                for c in range(2)]
    return pl.pallas_call(
        functools.partial(_attn_sample_kernel, n_heads=n_heads, d=d, tk=tk, past=past, lam_init=lam_init),
        grid=(b, nkb + 1),
        in_specs=[new, ck_spec, *cv_specs, new, new, vec, vec, vec, vec,
                  pl.BlockSpec((1, hw), lambda bi, kb: (0, 0))],
        out_specs=new,
        out_shape=jax.ShapeDtypeStruct((b, tq, width), BF16),
        scratch_shapes=[
            pltpu.VMEM((2 * n_heads * tq, 2 * tk), F32),
            pltpu.VMEM((2 * n_heads * tq, 2 * tk), BF16),
            pltpu.VMEM((2 * n_heads * tq, STAT_LANES), F32),
            pltpu.VMEM((2 * n_heads * tq, STAT_LANES), F32),
            pltpu.VMEM((2 * n_heads * tq, hw), F32),
        ],
        compiler_params=_params("parallel", "arbitrary"),
        name="attn_sample",
    )(q, cache_k, cache_v, cache_v, k_new, v_new, *lam_vecs, g_subln)


def _pool_mix_kernel(u_ref, prev_ref, hist_ref, wp_ref, sc_ref, o_ref, ext_ref, *, tt, pos0):
    t = pl.program_id(1)
    ext_ref[0:HIST_ROWS, :] = jnp.where(t == 0, hist_ref[...], prev_ref[...])
    ext_ref[HIST_ROWS:HIST_ROWS + tt, :] = u_ref[...]
    pos = pos0 + t * tt + lax.broadcasted_iota(jnp.int32, (tt, 1), 0)
    gw = u_ref.shape[1] // len(POOL_WINDOWS)
    for g, w in enumerate(POOL_WINDOWS):
        cols = slice(g * gw, (g + 1) * gw)
        win = ext_ref[HIST_ROWS:HIST_ROWS + tt, cols]
        for back in range(1, w):
            win = win + ext_ref[HIST_ROWS - back:HIST_ROWS - back + tt, cols]
        count = jnp.minimum(pos + 1, w).astype(F32)
        pooled = win / count - u_ref[:, cols]
        mixed = _dot(pooled.astype(BF16), wp_ref[g]) * sc_ref[:, cols]
        o_ref[:, cols] = mixed.astype(o_ref.dtype)


def _pool_mix(u, hist, w_pool, pool_scale, pos0):
    b, t, width = u.shape
    tt = _tile(t, 256)
    assert tt % HIST_ROWS == 0
    hist16 = jnp.pad(hist, ((0, 0), (HIST_ROWS - POOL_HIST, 0), (0, 0)))
    per = tt // HIST_ROWS
    return pl.pallas_call(
        functools.partial(_pool_mix_kernel, tt=tt, pos0=pos0),
        grid=(b, t // tt),
        in_specs=[
            pl.BlockSpec((None, tt, width), lambda bi, ti: (bi, ti, 0)),
            pl.BlockSpec((None, HIST_ROWS, width), lambda bi, ti: (bi, jnp.maximum(ti * per - 1, 0), 0)),
            pl.BlockSpec((None, HIST_ROWS, width), lambda bi, ti: (bi, 0, 0)),
            pl.BlockSpec(w_pool.shape, lambda bi, ti: (0, 0, 0)),
            pl.BlockSpec((1, width), lambda bi, ti: (0, 0)),
        ],
        out_specs=pl.BlockSpec((None, tt, width), lambda bi, ti: (bi, ti, 0)),
        out_shape=jax.ShapeDtypeStruct((b, t, width), BF16),
        scratch_shapes=[pltpu.VMEM((HIST_ROWS + tt, width), F32)],
        compiler_params=_params("parallel", "parallel"),
        name="pool_mix",
    )(u, u, hist16, w_pool, pool_scale)


def _trunk_layer(x, pos0, pool_hist, kv_cache, lam_init, w, wb):
    b, t, dm = x.shape
    m = b * t
    n_heads, d = w["n_heads"], w["head_dim"]
    qk_w = n_heads * 2 * d
    attn_w, pool_w = qk_w, w["w_pool_out"].shape[0]
    c_k, c_v, c_u, c_g = qk_w, 2 * qk_w, 2 * qk_w + attn_w, 2 * qk_w + attn_w + pool_w
    x2 = x.reshape(m, dm)
    w_in = w["w_in"]
    host = wb is None
    if host:
        wb = {"q": w_in[:, :qk_w].astype(BF16), "pool": w["w_pool"].astype(BF16)}

    def call(names, casts, *args):
        res = _tiled_call(*args, casts=casts if host else ())
        n_main = len(res) - (len(names) if host else 0)
        if host:
            wb.update(zip(names, res[n_main:]))
        return res[:n_main]

    xn = _rmsnorm_cast(x2, w["g_norm1"], NORM_EPS)
    (q,) = call(("k", "v", "u"), [(w_in, c_k, qk_w), (w_in, c_v, attn_w), (w_in, c_u, pool_w)],
                functools.partial(_q_body, scale=d ** -0.5 * LOG2E), [xn], [(wb["q"], 0)], [], [w["g_q"]], [BF16], qk_w,
                1024, 1024, "proj_q")
    k_f, k_b = call(("attn_out", "g"), [(w["w_attn_out"], 0, dm), (w_in, c_g, 2 * dm)],
                    functools.partial(_k_body, groups=qk_w // LANES), [xn], [(wb["k"], 0)], [], [w["g_k"]],
                    [("rows", F32), BF16], qk_w, 512, 1024, "proj_k")
    v_f, v_b = call(("pool_out",), [(w["w_pool_out"], 0, dm)],
                    _v_body, [xn], [(wb["v"], 0)], [], [], [F32, BF16], attn_w, 1024, 1024, "proj_v")
    (u,) = _tiled_call(_u_body, [xn], [(wb["u"], 0)], [], [], [F32], pool_w, 1024, 1024, "proj_u")
    (gates,) = call(("up", "o"), [(w["w_up"], 0, w["w_up"].shape[1]), (w["w_o"], 0, dm)],
                    _gate_body, [xn], [(wb["g"], 0)], [], [], [BF16], 2 * dm, 1024, 1024, "proj_gates")

    lam_vecs = w["lam_vecs"]
    if kv_cache is None:
        o = _attn_prompt(q.reshape(b, t, qk_w), k_b.reshape(b, t, qk_w), v_b.reshape(b, t, attn_w),
                         lam_vecs, w["g_subln"], n_heads, lam_init)
    else:
        ck, cv = kv_cache
        o = _attn_sample(q.reshape(b, t, qk_w), ck, cv, k_b.reshape(b, t, qk_w), v_b.reshape(b, t, attn_w),
                         lam_vecs, w["g_subln"], n_heads, lam_init)

    u3 = u.reshape(b, t, pool_w)
    yb_in = _pool_mix(u3, pool_hist, wb["pool"], w["pool_scale"], pos0)
    assert t >= POOL_HIST
    pool_state = u3[:, t - POOL_HIST:, :]

    (merged,) = _tiled_call(_merge_body, [o.reshape(m, attn_w), yb_in.reshape(m, pool_w)],
                            [(wb["attn_out"], 0), (wb["pool_out"], 0)], [(gates, 0), (gates, dm)], [],
                            [BF16], dm, 1024, 1024, "merge")
    x1, x1g, ssq = _tiled_call(_residual_norm_body, [merged], [(wb["o"], 0)], [(x2, 0)],
                               [("cols", w["g_norm2"])], [F32, BF16, ("stat", F32)], dm, 1024, 512, "out_proj")
    (act,) = call(("down",), [(w["w_down"], 0, dm)],
                  functools.partial(_relu2_norm_body, dim=dm), [x1g, ssq], [(wb["up"], 0)], [], [], [BF16],
                  w["w_up"].shape[1], 1024, 1024, "mlp_up")
    y = _down_proj(act, wb["down"], x1, 1024, 1024, 4096)
    return (y.reshape(b, t, dm), k_f.reshape(b, t, n_heads, 2, d), v_f.reshape(b, t, n_heads, 2 * d),
            pool_state), wb


def kernel(x_prompt, x_sample, cache_k, cache_v, state_pool, g_norm1, w_in, g_q, g_k, lambda_q1, lambda_k1,
           lambda_q2, lambda_k2, g_subln, w_attn_out, w_pool, pool_scale, w_pool_out, w_o, g_norm2, w_up,
           w_down):
    depth, dec_b, past, n_heads, _, d = cache_k.shape
    y_p, y_s = x_prompt, x_sample
    hist_p = jnp.zeros((x_prompt.shape[0], POOL_HIST, w_pool_out.shape[1]), F32)
    outs = [[] for _ in range(6)]
    for l in range(depth):
        lam_init = 0.8 - 0.6 * math.exp(-0.3 * l)
        row = lambda a: a[l].reshape(1, -1).astype(F32)
        w = dict(
            n_heads=n_heads, head_dim=d,
            g_norm1=g_norm1[l], g_norm2=row(g_norm2), g_q=row(g_q), g_k=row(g_k), g_subln=row(g_subln),
            lam_vecs=(row(lambda_q1), row(lambda_k1), row(lambda_q2), row(lambda_k2)),
            pool_scale=row(pool_scale),
            w_in=w_in[l], w_attn_out=w_attn_out[l], w_pool=w_pool[l], w_pool_out=w_pool_out[l], w_o=w_o[l],
            w_up=w_up[l], w_down=w_down[l],
        )
        (y_p, kp, vp, sp), wb = _trunk_layer(y_p, 0, hist_p, None, lam_init, w, None)
        kv = (cache_k[l].reshape(dec_b, past * n_heads * 2, d), cache_v[l].reshape(dec_b, past * n_heads, 2 * d))
        (y_s, ks, vs, ss), _ = _trunk_layer(y_s, past, state_pool[l], kv, lam_init, w, wb)
        for lst, val in zip(outs, (kp, vp, sp, ks, vs, ss)):
            lst.append(val)
    return (y_p, y_s) + tuple(jnp.stack(o) for o in outs)
```

```python
import functools
import math

import jax
import jax.numpy as jnp
from jax import lax
from jax.experimental import pallas as pl
from jax.experimental.pallas import tpu as pltpu

CHUNK = 64
POOL_WINDOWS = (2, 4, 8, 16)
POOL_HIST = max(POOL_WINDOWS) - 1
HIST_ROWS = 16
NORM_EPS = 1e-6
SUBLN_EPS = 1e-5
NEG_INF = -1e30
LOG2E = math.log2(math.e)
V7X_VMEM_LIMIT_BYTES = 56 * 1024 * 1024
LANES = 128
STAT_LANES = LANES
BF16_SUBLANES = 16

F32 = jnp.float32
BF16 = jnp.bfloat16


def _tile(dim, pref):
    t = min(dim, pref)
    while dim % t:
        t -= 1
    return t


def _params(*sem):
    return pltpu.CompilerParams(dimension_semantics=sem, vmem_limit_bytes=V7X_VMEM_LIMIT_BYTES)


def _dot(a, b):
    return jnp.dot(a, b, preferred_element_type=F32)


def _dot_nt(a, b):
    return lax.dot_general(a, b, (((1,), (1,)), ((), ())), preferred_element_type=F32)


def _rmsnorm_cast_kernel(x_ref, g_ref, o_ref, *, eps):
    x = x_ref[...]
    ms = jnp.mean(x * x, axis=-1, keepdims=True)
    o_ref[...] = (x * lax.rsqrt(ms + eps) * g_ref[...]).astype(o_ref.dtype)


def _rmsnorm_cast(x, g, eps):
    m, d = x.shape
    tm = _tile(m, 256)
    return pl.pallas_call(
        functools.partial(_rmsnorm_cast_kernel, eps=eps),
        grid=(m // tm,),
        in_specs=[pl.BlockSpec((tm, d), lambda i: (i, 0)), pl.BlockSpec((1, d), lambda i: (0, 0))],
        out_specs=pl.BlockSpec((tm, d), lambda i: (i, 0)),
        out_shape=jax.ShapeDtypeStruct((m, d), BF16),
        compiler_params=_params("parallel"),
        name="rmsnorm_cast",
    )(x, g.reshape(1, d).astype(F32))


def _with_casts(body, n_in, n_out, n_cast):
    def wrapped(*refs):
        main_in, cast_in = refs[:n_in], refs[n_in:n_in + n_cast]
        outs = refs[n_in + n_cast:]
        body(*main_in, *outs[:n_out])
        for src, dst in zip(cast_in, outs[n_out:]):
            dst[...] = src[...].astype(dst.dtype)
    return wrapped


def _tiled_call(body, rows, weights, tiles, vecs, out_dtypes, n_cols, tm, tn, name, casts=()):
    m = rows[0].shape[0]
    tm = _tile(m, tm)
    tn = _tile(n_cols, tn)
    steps = (m // tm) * (n_cols // tn)
    nj = n_cols // tn
    hosted = [c for c in casts if c[0].shape[0] % (steps * BF16_SUBLANES) == 0 and c[1] % c[2] == 0]
    if len(hosted) != len(casts):
        res = _tiled_call(body, rows, weights, tiles, vecs, out_dtypes, n_cols, tm, tn, name)
        return list(res) + [c[0][:, c[1]:c[1] + c[2]].astype(BF16) for c in casts]
    in_specs, args = [], []
    for a in rows:
        in_specs.append(pl.BlockSpec((tm, a.shape[1]), lambda i, j: (i, 0)))
        args.append(a)
    for w, c0 in weights:
        assert c0 % tn == 0
        in_specs.append(pl.BlockSpec((w.shape[0], tn), lambda i, j, o=c0 // tn: (0, j + o)))
        args.append(w)
    for t, c0 in tiles:
        assert c0 % tn == 0
        in_specs.append(pl.BlockSpec((tm, tn), lambda i, j, o=c0 // tn: (i, j + o)))
        args.append(t)
    for v in vecs:
        if isinstance(v, tuple):
            in_specs.append(pl.BlockSpec((1, tn), lambda i, j: (0, j)))
            args.append(v[1])
        else:
            in_specs.append(pl.BlockSpec(v.shape, lambda i, j: (0, 0)))
            args.append(v)
    n_in = len(args)
    out_specs, out_shape, j_sem = [], [], "parallel"
    for dt in out_dtypes:
        if not isinstance(dt, tuple):
            out_specs.append(pl.BlockSpec((tm, tn), lambda i, j: (i, j)))
            out_shape.append(jax.ShapeDtypeStruct((m, n_cols), dt))
        elif dt[0] == "rows":
            groups = n_cols // LANES
            out_specs.append(pl.BlockSpec((tm * groups, LANES), lambda i, j: (i, 0)))
            out_shape.append(jax.ShapeDtypeStruct((m * groups, LANES), dt[1]))
            j_sem = "arbitrary"
        else:
            out_specs.append(pl.BlockSpec((tm, LANES), lambda i, j: (i, j)))
            out_shape.append(jax.ShapeDtypeStruct((m, nj * LANES), dt[1]))
    for src, c0, n in casts:
        slab = src.shape[0] // steps
        in_specs.append(pl.BlockSpec((slab, n), lambda i, j, o=c0 // n: (i * nj + j, o)))
        args.append(src)
        out_specs.append(pl.BlockSpec((slab, n), lambda i, j: (i * nj + j, 0)))
        out_shape.append(jax.ShapeDtypeStruct((src.shape[0], n), BF16))
    if casts:
        body = _with_casts(body, n_in, len(out_dtypes), len(casts))
    return pl.pallas_call(
        body,
        grid=(m // tm, n_cols // tn),
        in_specs=in_specs,
        out_specs=out_specs,
        out_shape=out_shape,
        compiler_params=_params("parallel", j_sem),
        name=name,
    )(*args)


def _head_rmsnorm_store(acc, g, eps, scale, out_refs):
    group = g.shape[1]
    for c in range(acc.shape[1] // group):
        sl = slice(c * group, (c + 1) * group)
        blk = acc[:, sl]
        ms = jnp.mean(blk * blk, axis=-1, keepdims=True)
        y = blk * lax.rsqrt(ms + eps) * g
        for ref, s in zip(out_refs, scale):
            ref[:, sl] = (y if s == 1.0 else y * s).astype(ref.dtype)


def _q_body(a_ref, w_ref, g_ref, q_ref, *, scale):
    _head_rmsnorm_store(_dot(a_ref[...], w_ref[...]), g_ref[...], NORM_EPS, (scale,), (q_ref,))


def _k_body(a_ref, w_ref, g_ref, kf_ref, kb_ref, *, groups):
    acc = _dot(a_ref[...], w_ref[...])
    g = g_ref[...]
    tm, tn = acc.shape
    first = pl.program_id(1) * (tn // LANES)
    for c in range(tn // LANES):
        blk = acc[:, c * LANES:(c + 1) * LANES]
        ms = jnp.mean(blk * blk, axis=-1, keepdims=True)
        y = blk * lax.rsqrt(ms + NORM_EPS) * g
        kf_ref[pl.ds(first + c, tm, stride=groups), :] = y
        kb_ref[:, c * LANES:(c + 1) * LANES] = y.astype(kb_ref.dtype)


def _v_body(a_ref, w_ref, vf_ref, vb_ref):
    acc = _dot(a_ref[...], w_ref[...])
    vf_ref[...] = acc
    vb_ref[...] = acc.astype(vb_ref.dtype)


def _u_body(a_ref, w_ref, u_ref):
    u_ref[...] = _dot(a_ref[...], w_ref[...])


def _gate_body(a_ref, w_ref, g_ref):
    g_ref[...] = jax.nn.sigmoid(_dot(a_ref[...], w_ref[...])).astype(g_ref.dtype)


def _merge_body(o_ref, yb_ref, wa_ref, wp_ref, ga_ref, gb_ref, out_ref):
    ya = _dot(o_ref[...], wa_ref[...])
    yb = _dot(yb_ref[...], wp_ref[...])
    out_ref[...] = (ga_ref[...].astype(F32) * ya + gb_ref[...].astype(F32) * yb).astype(out_ref.dtype)


def _residual_norm_body(a_ref, w_ref, x_ref, g_ref, out_ref, xg_ref, ssq_ref):
    x1 = x_ref[...] + _dot(a_ref[...], w_ref[...])
    out_ref[...] = x1
    xg_ref[...] = (x1 * g_ref[...]).astype(xg_ref.dtype)
    ssq_ref[...] = jnp.broadcast_to(jnp.sum(x1 * x1, axis=-1, keepdims=True), ssq_ref.shape)


def _relu2_norm_body(a_ref, ssq_ref, w_ref, out_ref, *, dim):
    ssq = ssq_ref[...]
    tot = ssq[:, :LANES]
    for c in range(1, ssq.shape[1] // LANES):
        tot = tot + ssq[:, c * LANES:(c + 1) * LANES]
    r = lax.rsqrt(tot * (1.0 / dim) + NORM_EPS)
    acc = _dot(a_ref[...], w_ref[...])
    h = jnp.maximum(acc * _rep(r, acc.shape[1]), 0.0)
    out_ref[...] = (h * h).astype(out_ref.dtype)


def _down_body(a_ref, w_ref, x_ref, out_ref):
    @pl.when(pl.program_id(2) == 0)
    def _():
        out_ref[...] = x_ref[...]

    out_ref[...] += _dot(a_ref[...], w_ref[...])


def _down_proj(a, w, x, tm, tn, tk):
    m, kdim = a.shape
    n = w.shape[1]
    tm, tn, tk = _tile(m, tm), _tile(n, tn), _tile(kdim, tk)
    return pl.pallas_call(
        _down_body,
        grid=(m // tm, n // tn, kdim // tk),
        in_specs=[
            pl.BlockSpec((tm, tk), lambda i, j, k: (i, k)),
            pl.BlockSpec((tk, tn), lambda i, j, k: (k, j)),
            pl.BlockSpec((tm, tn), lambda i, j, k: (i, j)),
        ],
        out_specs=pl.BlockSpec((tm, tn), lambda i, j, k: (i, j)),
        out_shape=jax.ShapeDtypeStruct((m, n), F32),
        compiler_params=_params("parallel", "parallel", "arbitrary"),
        name="down_proj",
    )(a, w, x)


def _rep(x, width):
    if width <= STAT_LANES:
        return x[:, :width]
    return jnp.concatenate([x] * (width // STAT_LANES), axis=1)


def _lambda(lq1, lk1, lq2, lk2, lam_init):
    return (jnp.exp(jnp.sum(lq1[...] * lk1[...], keepdims=True))
            - jnp.exp(jnp.sum(lq2[...] * lk2[...], keepdims=True)) + lam_init)


def _subln(o, gs, lam_init):
    ms = jnp.mean(o * o, axis=-1, keepdims=True)
    return o * lax.rsqrt(ms + SUBLN_EPS) * gs * (1.0 - lam_init)


def _attn_prompt_kernel(q_ref, k_ref, v_ref, lq1, lk1, lq2, lk2, gs_ref, o_ref, *, tq, d, lam_init):
    t = q_ref.shape[0]
    lam = _lambda(lq1, lk1, lq2, lk2, lam_init)
    gs = gs_ref[...]
    rq = lax.broadcasted_iota(jnp.int32, (tq, tq), 0) // CHUNK
    rk = lax.broadcasted_iota(jnp.int32, (tq, tq), 1) // CHUNK
    diag_mask = rk <= rq
    for i in range(t // tq):
        r0 = i * tq
        w_diag, w_prev = None, None
        for c in range(2):
            qc = q_ref[r0:r0 + tq, c * d:(c + 1) * d]
            s_diag = jnp.where(diag_mask, _dot_nt(qc, k_ref[r0:r0 + tq, c * d:(c + 1) * d]), NEG_INF)
            m = jnp.max(s_diag, axis=-1, keepdims=True)
            if i > 0:
                s_prev = _dot_nt(qc, k_ref[0:r0, c * d:(c + 1) * d])
                m = jnp.maximum(m, jnp.max(s_prev, axis=-1, keepdims=True))
            p_diag = jnp.exp2(s_diag - m)
            l = jnp.sum(p_diag, axis=-1, keepdims=True)
            if i > 0:
                p_prev = jnp.exp2(s_prev - m)
                l = l + jnp.sum(p_prev, axis=-1, keepdims=True)
            if c == 0:
                l1, w_diag = l, p_diag
                if i > 0:
                    w_prev = p_prev
            else:
                ratio = -lam * l1 / l
                w_diag = w_diag + p_diag * ratio
                if i > 0:
                    w_prev = w_prev + p_prev * ratio
        o = _dot(w_diag.astype(BF16), v_ref[r0:r0 + tq, :])
        if i > 0:
            o = o + _dot(w_prev.astype(BF16), v_ref[0:r0, :])
        o_ref[r0:r0 + tq, :] = _subln(o / l1, gs, lam_init).astype(o_ref.dtype)


def _attn_prompt(q, k, v, lam_vecs, g_subln, n_heads, lam_init, casts=()):
    b, t, width = q.shape
    hw = width // n_heads
    d = hw // 2
    tq = _tile(t, 256)
    assert tq % CHUNK == 0
    steps = b * n_heads
    if any(c[0].shape[0] % (steps * BF16_SUBLANES) or c[1] % c[2] for c in casts):
        (o,) = _attn_prompt(q, k, v, lam_vecs, g_subln, n_heads, lam_init)
        return [o] + [c[0][:, c[1]:c[1] + c[2]].astype(BF16) for c in casts]
    vec = pl.BlockSpec((1, d), lambda bi, h: (0, 0))
    seq = pl.BlockSpec((None, t, hw), lambda bi, h: (bi, 0, h))
    in_specs = [seq, seq, seq, vec, vec, vec, vec, pl.BlockSpec((1, hw), lambda bi, h: (0, 0))]
    args = [q, k, v, *lam_vecs, g_subln]
    out_specs, out_shape = [seq], [jax.ShapeDtypeStruct((b, t, width), BF16)]
    body = functools.partial(_attn_prompt_kernel, tq=tq, d=d, lam_init=lam_init)
    if casts:
        body = _with_casts(body, len(args), 1, len(casts))
    for src, c0, n in casts:
        slab = src.shape[0] // steps
        in_specs.append(pl.BlockSpec((slab, n), lambda bi, h, o=c0 // n: (bi * n_heads + h, o)))
        args.append(src)
        out_specs.append(pl.BlockSpec((slab, n), lambda bi, h: (bi * n_heads + h, 0)))
        out_shape.append(jax.ShapeDtypeStruct((src.shape[0], n), BF16))
    return pl.pallas_call(
        body,
        grid=(b, n_heads),
        in_specs=in_specs,
        out_specs=out_specs,
        out_shape=out_shape,
        compiler_params=_params("parallel", "parallel"),
        name="attn_prompt",
    )(*args)


def _attn_sample_kernel(q_ref, ck_ref, cv0_ref, cv1_ref, kn_ref, vn_ref, lq1, lk1, lq2, lk2, gs_ref, o_ref,
                        s_sc, p_sc, m_sc, l_sc, acc_sc, *, n_heads, d, tk, past, lam_init):
    kb = pl.program_id(1)
    nkb = pl.num_programs(1) - 1
    hw = 2 * d
    tq = q_ref.shape[0]
    half = n_heads // 2

    def slot(h, c):
        return (h % half) * 4 + (h // half) * 2 + c

    def rows(first_slot, n_slots=1):
        return slice(first_slot * tq, (first_slot + n_slots) * tq)

    def q_of(h, c):
        return q_ref[:, h * hw + c * d:h * hw + (c + 1) * d]

    @pl.when(kb == 0)
    def _():
        m_sc[...] = jnp.full(m_sc.shape, -jnp.inf, F32)
        l_sc[...] = jnp.zeros(l_sc.shape, F32)
        acc_sc[...] = jnp.zeros(acc_sc.shape, F32)

    def softmax_update(width):
        s = s_sc[:, :width]
        m_old = m_sc[...]
        m_new = jnp.maximum(m_old, jnp.max(s, axis=-1, keepdims=True))
        alpha = jnp.exp2(m_old - m_new)
        p = jnp.exp2(s - _rep(m_new, width))
        l_sc[...] = alpha * l_sc[...] + jnp.sum(p, axis=-1, keepdims=True)
        m_sc[...] = m_new
        p_sc[:, :width] = p.astype(p_sc.dtype)
        return alpha

    @pl.when(kb == 0)
    def _():
        rq = (past + lax.broadcasted_iota(jnp.int32, (tq, tq), 0)) // CHUNK
        rk = (past + lax.broadcasted_iota(jnp.int32, (tq, tq), 1)) // CHUNK
        for h in range(n_heads):
            for c in range(2):
                s = _dot_nt(q_of(h, c), kn_ref[:, h * hw + c * d:h * hw + (c + 1) * d])
                s_sc[rows(slot(h, c)), :tq] = jnp.where(rk <= rq, s, NEG_INF)
        alpha = softmax_update(tq)
        for h in range(n_heads):
            r = rows(slot(h, 0), 2)
            acc_sc[r, :] = _rep(alpha[r], hw) * acc_sc[r, :] + _dot(p_sc[r, :tq], vn_ref[:, h * hw:(h + 1) * hw])

    @pl.when(kb > 0)
    def _():
        odd = lax.broadcasted_iota(jnp.int32, (tq, 2 * tk), 1) % 2 == 1
        for h in range(half):
            for c in range(2):
                k_pair = ck_ref[pl.ds(2 * h + c, 2 * tk, stride=n_heads), :].astype(BF16)
                s = _dot_nt(jnp.concatenate([q_of(h, c), q_of(h + half, c)], axis=0), k_pair)
                s_sc[rows(slot(h, c)), :] = jnp.where(odd, NEG_INF, s[:tq])
                s_sc[rows(slot(h + half, c)), :] = jnp.where(odd, s[tq:], NEG_INF)
        alpha = softmax_update(2 * tk)
        for h in range(half):
            v_pair = jnp.concatenate([cv0_ref[pl.ds(h, 2 * tk, stride=half), :],
                                      cv1_ref[pl.ds(h, 2 * tk, stride=half), :]], axis=1).astype(BF16)
            r = rows(slot(h, 0), 4)
            acc_sc[r, :] = _rep(alpha[r], hw) * acc_sc[r, :] + _dot(p_sc[r, :], v_pair)

    @pl.when(kb == nkb)
    def _():
        lam = _lambda(lq1, lk1, lq2, lk2, lam_init)
        for h in range(n_heads):
            r1, r2 = rows(slot(h, 0)), rows(slot(h, 1))
            o = acc_sc[r1, :] / _rep(l_sc[r1, :], hw) - lam * (acc_sc[r2, :] / _rep(l_sc[r2, :], hw))
            o_ref[:, h * hw:(h + 1) * hw] = _subln(o, gs_ref[...], lam_init).astype(o_ref.dtype)


def _attn_sample(q, cache_k, cache_v, k_new, v_new, lam_vecs, g_subln, n_heads, lam_init):
    b, tq, width = q.shape
    hw = width // n_heads
    d = hw // 2
    assert d == STAT_LANES and n_heads % 2 == 0
    past = cache_v.shape[1] // n_heads
    tk = _tile(past, 512)
    nkb = past // tk
    vec = pl.BlockSpec((1, d), lambda bi, kb: (0, 0))
    new = pl.BlockSpec((None, tq, width), lambda bi, kb: (bi, 0, 0))
    ck_spec = pl.BlockSpec((None, tk * 2 * n_heads, d), lambda bi, kb: (bi, jnp.maximum(kb - 1, 0), 0))
    cv_specs = [pl.BlockSpec((None, tk * n_heads, d), lambda bi, kb, c=c: (bi, jnp.maximum(kb - 1, 0), c))
                for c in range(2)]
    return pl.pallas_call(
        functools.partial(_attn_sample_kernel, n_heads=n_heads, d=d, tk=tk, past=past, lam_init=lam_init),
        grid=(b, nkb + 1),
        in_specs=[new, ck_spec, *cv_specs, new, new, vec, vec, vec, vec,
                  pl.BlockSpec((1, hw), lambda bi, kb: (0, 0))],
        out_specs=new,
        out_shape=jax.ShapeDtypeStruct((b, tq, width), BF16),
        scratch_shapes=[
            pltpu.VMEM((2 * n_heads * tq, 2 * tk), F32),
            pltpu.VMEM((2 * n_heads * tq, 2 * tk), BF16),
            pltpu.VMEM((2 * n_heads * tq, STAT_LANES), F32),
            pltpu.VMEM((2 * n_heads * tq, STAT_LANES), F32),
            pltpu.VMEM((2 * n_heads * tq, hw), F32),
        ],
        compiler_params=_params("parallel", "arbitrary"),
        name="attn_sample",
    )(q, cache_k, cache_v, cache_v, k_new, v_new, *lam_vecs, g_subln)


def _pool_mix_kernel(u_ref, prev_ref, hist_ref, wp_ref, sc_ref, o_ref, ext_ref, *, tt, pos0):
    t = pl.program_id(1)
    ext_ref[0:HIST_ROWS, :] = jnp.where(t == 0, hist_ref[...], prev_ref[...])
    ext_ref[HIST_ROWS:HIST_ROWS + tt, :] = u_ref[...]
    pos = pos0 + t * tt + lax.broadcasted_iota(jnp.int32, (tt, 1), 0)
    gw = u_ref.shape[1] // len(POOL_WINDOWS)
    for g, w in enumerate(POOL_WINDOWS):
        cols = slice(g * gw, (g + 1) * gw)
        win = ext_ref[HIST_ROWS:HIST_ROWS + tt, cols]
        for back in range(1, w):
            win = win + ext_ref[HIST_ROWS - back:HIST_ROWS - back + tt, cols]
        count = jnp.minimum(pos + 1, w).astype(F32)
        pooled = win / count - u_ref[:, cols]
        mixed = _dot(pooled.astype(BF16), wp_ref[g]) * sc_ref[:, cols]
        o_ref[:, cols] = mixed.astype(o_ref.dtype)


def _pool_mix(u, hist, w_pool, pool_scale, pos0):
    b, t, width = u.shape
    tt = _tile(t, 256)
    assert tt % HIST_ROWS == 0
    hist16 = jnp.pad(hist, ((0, 0), (HIST_ROWS - POOL_HIST, 0), (0, 0)))
    per = tt // HIST_ROWS
    return pl.pallas_call(
        functools.partial(_pool_mix_kernel, tt=tt, pos0=pos0),
        grid=(b, t // tt),
        in_specs=[
            pl.BlockSpec((None, tt, width), lambda bi, ti: (bi, ti, 0)),
            pl.BlockSpec((None, HIST_ROWS, width), lambda bi, ti: (bi, jnp.maximum(ti * per - 1, 0), 0)),
            pl.BlockSpec((None, HIST_ROWS, width), lambda bi, ti: (bi, 0, 0)),
            pl.BlockSpec(w_pool.shape, lambda bi, ti: (0, 0, 0)),
            pl.BlockSpec((1, width), lambda bi, ti: (0, 0)),
        ],
        out_specs=pl.BlockSpec((None, tt, width), lambda bi, ti: (bi, ti, 0)),
        out_shape=jax.ShapeDtypeStruct((b, t, width), BF16),
        scratch_shapes=[pltpu.VMEM((HIST_ROWS + tt, width), F32)],
        compiler_params=_params("parallel", "parallel"),
        name="pool_mix",
    )(u, u, hist16, w_pool, pool_scale)


def _trunk_layer(x, pos0, pool_hist, kv_cache, lam_init, w, wb):
    b, t, dm = x.shape
    m = b * t
    n_heads, d = w["n_heads"], w["head_dim"]
    qk_w = n_heads * 2 * d
    attn_w, pool_w = qk_w, w["w_pool_out"].shape[0]
    c_k, c_v, c_u, c_g = qk_w, 2 * qk_w, 2 * qk_w + attn_w, 2 * qk_w + attn_w + pool_w
    x2 = x.reshape(m, dm)
    w_in = w["w_in"]
    host = wb is None
    if host:
        wb = {"q": w_in[:, :qk_w].astype(BF16), "pool": w["w_pool"].astype(BF16)}

    def call(names, casts, *args):
        res = _tiled_call(*args, casts=casts if host else ())
        n_main = len(res) - (len(names) if host else 0)
        if host:
            wb.update(zip(names, res[n_main:]))
        return res[:n_main]

    xn = _rmsnorm_cast(x2, w["g_norm1"], NORM_EPS)
    (q,) = call(("k", "v", "u"), [(w_in, c_k, qk_w), (w_in, c_v, attn_w), (w_in, c_u, pool_w)],
                functools.partial(_q_body, scale=d ** -0.5 * LOG2E), [xn], [(wb["q"], 0)], [], [w["g_q"]], [BF16], qk_w,
                1024, 1024, "proj_q")
    k_f, k_b = call(("attn_out",), [(w["w_attn_out"], 0, dm)],
                    functools.partial(_k_body, groups=qk_w // LANES), [xn], [(wb["k"], 0)], [], [w["g_k"]],
                    [("rows", F32), BF16], qk_w, 1024, 512, "proj_k")
    v_f, v_b = call(("pool_out",), [(w["w_pool_out"], 0, dm)],
                    _v_body, [xn], [(wb["v"], 0)], [], [], [F32, BF16], attn_w, 1024, 1024, "proj_v")
    (u,) = _tiled_call(_u_body, [xn], [(wb["u"], 0)], [], [], [F32], pool_w, 1024, 1024, "proj_u")

    lam_vecs = w["lam_vecs"]
    gate_cast = (w_in, c_g, 2 * dm)
    if kv_cache is None:
        res = _attn_prompt(q.reshape(b, t, qk_w), k_b.reshape(b, t, qk_w), v_b.reshape(b, t, attn_w),
                           lam_vecs, w["g_subln"], n_heads, lam_init, casts=[gate_cast] if host else ())
        o = res[0]
        if host:
            wb["g"] = res[1]
    else:
        if host:
            wb["g"] = w_in[:, c_g:c_g + 2 * dm].astype(BF16)
        ck, cv = kv_cache
        o = _attn_sample(q.reshape(b, t, qk_w), ck, cv, k_b.reshape(b, t, qk_w), v_b.reshape(b, t, attn_w),
                         lam_vecs, w["g_subln"], n_heads, lam_init)
    (gates,) = call(("up", "o"), [(w["w_up"], 0, w["w_up"].shape[1]), (w["w_o"], 0, dm)],
                    _gate_body, [xn], [(wb["g"], 0)], [], [], [BF16], 2 * dm, 1024, 1024, "proj_gates")

    u3 = u.reshape(b, t, pool_w)
    yb_in = _pool_mix(u3, pool_hist, wb["pool"], w["pool_scale"], pos0)
    assert t >= POOL_HIST
    pool_state = u3[:, t - POOL_HIST:, :]

    (merged,) = _tiled_call(_merge_body, [o.reshape(m, attn_w), yb_in.reshape(m, pool_w)],
                            [(wb["attn_out"], 0), (wb["pool_out"], 0)], [(gates, 0), (gates, dm)], [],
                            [BF16], dm, 1024, 1024, "merge")
    x1, x1g, ssq = _tiled_call(_residual_norm_body, [merged], [(wb["o"], 0)], [(x2, 0)],
                               [("cols", w["g_norm2"])], [F32, BF16, ("stat", F32)], dm, 1024, 512, "out_proj")
    (act,) = call(("down",), [(w["w_down"], 0, dm)],
                  functools.partial(_relu2_norm_body, dim=dm), [x1g, ssq], [(wb["up"], 0)], [], [], [BF16],
                  w["w_up"].shape[1], 1024, 1024, "mlp_up")
    y = _down_proj(act, wb["down"], x1, 1024, 1024, 4096)
    return (y.reshape(b, t, dm), k_f.reshape(b, t, n_heads, 2, d), v_f.reshape(b, t, n_heads, 2 * d),
            pool_state), wb


def kernel(x_prompt, x_sample, cache_k, cache_v, state_pool, g_norm1, w_in, g_q, g_k, lambda_q1, lambda_k1,
           lambda_q2, lambda_k2, g_subln, w_attn_out, w_pool, pool_scale, w_pool_out, w_o, g_norm2, w_up,
           w_down):
    depth, dec_b, past, n_heads, _, d = cache_k.shape
    y_p, y_s = x_prompt, x_sample
    hist_p = jnp.zeros((x_prompt.shape[0], POOL_HIST, w_pool_out.shape[1]), F32)
    outs = [[] for _ in range(6)]
    for l in range(depth):
        lam_init = 0.8 - 0.6 * math.exp(-0.3 * l)
        row = lambda a: a[l].reshape(1, -1).astype(F32)
        w = dict(
            n_heads=n_heads, head_dim=d,
            g_norm1=g_norm1[l], g_norm2=row(g_norm2), g_q=row(g_q), g_k=row(g_k), g_subln=row(g_subln),
            lam_vecs=(row(lambda_q1), row(lambda_k1), row(lambda_q2), row(lambda_k2)),
            pool_scale=row(pool_scale),
            w_in=w_in[l], w_attn_out=w_attn_out[l], w_pool=w_pool[l], w_pool_out=w_pool_out[l], w_o=w_o[l],
            w_up=w_up[l], w_down=w_down[l],
        )
        (y_p, kp, vp, sp), wb = _trunk_layer(y_p, 0, hist_p, None, lam_init, w, None)
        kv = (cache_k[l].reshape(dec_b, past * n_heads * 2, d), cache_v[l].reshape(dec_b, past * n_heads, 2 * d))
        (y_s, ks, vs, ss), _ = _trunk_layer(y_s, past, state_pool[l], kv, lam_init, w, wb)
        for lst, val in zip(outs, (kp, vp, sp, ks, vs, ss)):
            lst.append(val)
    return (y_p, y_s) + tuple(jnp.stack(o) for o in outs)
```

```python
import functools
import math

import jax
import jax.numpy as jnp
from jax import lax
from jax.experimental import pallas as pl
from jax.experimental.pallas import tpu as pltpu

CHUNK = 64
POOL_WINDOWS = (2, 4, 8, 16)
POOL_HIST = max(POOL_WINDOWS) - 1
HIST_ROWS = 16
NORM_EPS = 1e-6
SUBLN_EPS = 1e-5
NEG_INF = -1e30
LOG2E = math.log2(math.e)
V7X_VMEM_LIMIT_BYTES = 56 * 1024 * 1024
LANES = 128
STAT_LANES = LANES
BF16_SUBLANES = 16

F32 = jnp.float32
BF16 = jnp.bfloat16


def _tile(dim, pref):
    t = min(dim, pref)
    while dim % t:
        t -= 1
    return t


def _params(*sem):
    return pltpu.CompilerParams(dimension_semantics=sem, vmem_limit_bytes=V7X_VMEM_LIMIT_BYTES)


def _dot(a, b):
    return jnp.dot(a, b, preferred_element_type=F32)


def _dot_nt(a, b):
    return lax.dot_general(a, b, (((1,), (1,)), ((), ())), preferred_element_type=F32)


def _rmsnorm_cast_kernel(x_ref, g_ref, o_ref, *, eps):
    x = x_ref[...]
    ms = jnp.mean(x * x, axis=-1, keepdims=True)
    o_ref[...] = (x * lax.rsqrt(ms + eps) * g_ref[...]).astype(o_ref.dtype)


def _rmsnorm_cast(x, g, eps):
    m, d = x.shape
    tm = _tile(m, 256)
    return pl.pallas_call(
        functools.partial(_rmsnorm_cast_kernel, eps=eps),
        grid=(m // tm,),
        in_specs=[pl.BlockSpec((tm, d), lambda i: (i, 0)), pl.BlockSpec((1, d), lambda i: (0, 0))],
        out_specs=pl.BlockSpec((tm, d), lambda i: (i, 0)),
        out_shape=jax.ShapeDtypeStruct((m, d), BF16),
        compiler_params=_params("parallel"),
        name="rmsnorm_cast",
    )(x, g.reshape(1, d).astype(F32))


def _with_casts(body, n_in, n_out, n_cast):
    def wrapped(*refs):
        main_in, cast_in = refs[:n_in], refs[n_in:n_in + n_cast]
        outs = refs[n_in + n_cast:]
        body(*main_in, *outs[:n_out])
        for src, dst in zip(cast_in, outs[n_out:]):
            dst[...] = src[...].astype(dst.dtype)
    return wrapped


def _tiled_call(body, rows, weights, tiles, vecs, out_dtypes, n_cols, tm, tn, name, casts=()):
    m = rows[0].shape[0]
    tm = _tile(m, tm)
    tn = _tile(n_cols, tn)
    steps = (m // tm) * (n_cols // tn)
    nj = n_cols // tn
    hosted = [c for c in casts if c[0].shape[0] % (steps * BF16_SUBLANES) == 0 and c[1] % c[2] == 0]
    if len(hosted) != len(casts):
        res = _tiled_call(body, rows, weights, tiles, vecs, out_dtypes, n_cols, tm, tn, name)
        return list(res) + [c[0][:, c[1]:c[1] + c[2]].astype(BF16) for c in casts]
    in_specs, args = [], []
    for a in rows:
        in_specs.append(pl.BlockSpec((tm, a.shape[1]), lambda i, j: (i, 0)))
        args.append(a)
    for w, c0 in weights:
        assert c0 % tn == 0
        in_specs.append(pl.BlockSpec((w.shape[0], tn), lambda i, j, o=c0 // tn: (0, j + o)))
        args.append(w)
    for t, c0 in tiles:
        assert c0 % tn == 0
        in_specs.append(pl.BlockSpec((tm, tn), lambda i, j, o=c0 // tn: (i, j + o)))
        args.append(t)
    for v in vecs:
        if isinstance(v, tuple):
            in_specs.append(pl.BlockSpec((1, tn), lambda i, j: (0, j)))
            args.append(v[1])
        else:
            in_specs.append(pl.BlockSpec(v.shape, lambda i, j: (0, 0)))
            args.append(v)
    n_in = len(args)
    out_specs, out_shape, j_sem = [], [], "parallel"
    for dt in out_dtypes:
        if not isinstance(dt, tuple):
            out_specs.append(pl.BlockSpec((tm, tn), lambda i, j: (i, j)))
            out_shape.append(jax.ShapeDtypeStruct((m, n_cols), dt))
        elif dt[0] == "rows":
            groups = n_cols // LANES
            out_specs.append(pl.BlockSpec((tm * groups, LANES), lambda i, j: (i, 0)))
            out_shape.append(jax.ShapeDtypeStruct((m * groups, LANES), dt[1]))
            j_sem = "arbitrary"
        else:
            out_specs.append(pl.BlockSpec((tm, LANES), lambda i, j: (i, j)))
            out_shape.append(jax.ShapeDtypeStruct((m, nj * LANES), dt[1]))
    for src, c0, n in casts:
        slab = src.shape[0] // steps
        in_specs.append(pl.BlockSpec((slab, n), lambda i, j, o=c0 // n: (i * nj + j, o)))
        args.append(src)
        out_specs.append(pl.BlockSpec((slab, n), lambda i, j: (i * nj + j, 0)))
        out_shape.append(jax.ShapeDtypeStruct((src.shape[0], n), BF16))
    if casts:
        body = _with_casts(body, n_in, len(out_dtypes), len(casts))
    return pl.pallas_call(
        body,
        grid=(m // tm, n_cols // tn),
        in_specs=in_specs,
        out_specs=out_specs,
        out_shape=out_shape,
        compiler_params=_params("parallel", j_sem),
        name=name,
    )(*args)


def _head_rmsnorm_store(acc, g, eps, scale, out_refs):
    group = g.shape[1]
    for c in range(acc.shape[1] // group):
        sl = slice(c * group, (c + 1) * group)
        blk = acc[:, sl]
        ms = jnp.mean(blk * blk, axis=-1, keepdims=True)
        y = blk * lax.rsqrt(ms + eps) * g
        for ref, s in zip(out_refs, scale):
            ref[:, sl] = (y if s == 1.0 else y * s).astype(ref.dtype)


def _q_body(a_ref, w_ref, g_ref, q_ref, *, scale):
    _head_rmsnorm_store(_dot(a_ref[...], w_ref[...]), g_ref[...], NORM_EPS, (scale,), (q_ref,))


def _k_body(a_ref, w_ref, g_ref, kf_ref, kb_ref, *, groups):
    acc = _dot(a_ref[...], w_ref[...])
    g = g_ref[...]
    tm, tn = acc.shape
    first = pl.program_id(1) * (tn // LANES)
    for c in range(tn // LANES):
        blk = acc[:, c * LANES:(c + 1) * LANES]
        ms = jnp.mean(blk * blk, axis=-1, keepdims=True)
        y = blk * lax.rsqrt(ms + NORM_EPS) * g
        kf_ref[pl.ds(first + c, tm, stride=groups), :] = y
        kb_ref[:, c * LANES:(c + 1) * LANES] = y.astype(kb_ref.dtype)


def _v_body(a_ref, w_ref, vf_ref, vb_ref):
    acc = _dot(a_ref[...], w_ref[...])
    vf_ref[...] = acc
    vb_ref[...] = acc.astype(vb_ref.dtype)


def _u_body(a_ref, w_ref, u_ref):
    u_ref[...] = _dot(a_ref[...], w_ref[...])


def _gate_body(a_ref, w_ref, g_ref):
    acc = _dot(a_ref[...], w_ref[...])
    g_ref[...] = (0.5 * jnp.tanh(0.5 * acc) + 0.5).astype(g_ref.dtype)


def _merge_body(o_ref, yb_ref, wa_ref, wp_ref, ga_ref, gb_ref, out_ref):
    ya = _dot(o_ref[...], wa_ref[...])
    yb = _dot(yb_ref[...], wp_ref[...])
    out_ref[...] = (ga_ref[...].astype(F32) * ya + gb_ref[...].astype(F32) * yb).astype(out_ref.dtype)


def _residual_norm_body(a_ref, w_ref, x_ref, g_ref, out_ref, xg_ref, ssq_ref):
    x1 = x_ref[...] + _dot(a_ref[...], w_ref[...])
    out_ref[...] = x1
    xg_ref[...] = (x1 * g_ref[...]).astype(xg_ref.dtype)
    ssq_ref[...] = jnp.broadcast_to(jnp.sum(x1 * x1, axis=-1, keepdims=True), ssq_ref.shape)


def _relu2_norm_body(a_ref, ssq_ref, w_ref, out_ref, *, dim):
    ssq = ssq_ref[...]
    tot = ssq[:, :LANES]
    for c in range(1, ssq.shape[1] // LANES):
        tot = tot + ssq[:, c * LANES:(c + 1) * LANES]
    r = lax.rsqrt(tot * (1.0 / dim) + NORM_EPS)
    acc = _dot(a_ref[...], w_ref[...])
    h = jnp.maximum(acc * _rep(r, acc.shape[1]), 0.0)
    out_ref[...] = (h * h).astype(out_ref.dtype)


def _down_body(a_ref, w_ref, x_ref, out_ref):
    @pl.when(pl.program_id(2) == 0)
    def _():
        out_ref[...] = x_ref[...]

    out_ref[...] += _dot(a_ref[...], w_ref[...])


def _down_proj(a, w, x, tm, tn, tk):
    m, kdim = a.shape
    n = w.shape[1]
    tm, tn, tk = _tile(m, tm), _tile(n, tn), _tile(kdim, tk)
    return pl.pallas_call(
        _down_body,
        grid=(m // tm, n // tn, kdim // tk),
        in_specs=[
            pl.BlockSpec((tm, tk), lambda i, j, k: (i, k)),
            pl.BlockSpec((tk, tn), lambda i, j, k: (k, j)),
            pl.BlockSpec((tm, tn), lambda i, j, k: (i, j)),
        ],
        out_specs=pl.BlockSpec((tm, tn), lambda i, j, k: (i, j)),
        out_shape=jax.ShapeDtypeStruct((m, n), F32),
        compiler_params=_params("parallel", "parallel", "arbitrary"),
        name="down_proj",
    )(a, w, x)


def _rep(x, width):
    if width <= STAT_LANES:
        return x[:, :width]
    return jnp.concatenate([x] * (width // STAT_LANES), axis=1)


def _lambda(lq1, lk1, lq2, lk2, lam_init):
    return (jnp.exp(jnp.sum(lq1[...] * lk1[...], keepdims=True))
            - jnp.exp(jnp.sum(lq2[...] * lk2[...], keepdims=True)) + lam_init)


def _subln(o, gs, lam_init):
    ms = jnp.mean(o * o, axis=-1, keepdims=True)
    return o * lax.rsqrt(ms + SUBLN_EPS) * gs * (1.0 - lam_init)


def _attn_prompt_kernel(q_ref, k_ref, v_ref, lq1, lk1, lq2, lk2, gs_ref, o_ref, *, tq, d, lam_init):
    t = q_ref.shape[0]
    lam = _lambda(lq1, lk1, lq2, lk2, lam_init)
    gs = gs_ref[...]
    rq = lax.broadcasted_iota(jnp.int32, (tq, tq), 0) // CHUNK
    rk = lax.broadcasted_iota(jnp.int32, (tq, tq), 1) // CHUNK
    diag_mask = rk <= rq
    for i in range(t // tq):
        r0 = i * tq
        w_diag, w_prev = None, None
        for c in range(2):
            qc = q_ref[r0:r0 + tq, c * d:(c + 1) * d]
            s_diag = jnp.where(diag_mask, _dot_nt(qc, k_ref[r0:r0 + tq, c * d:(c + 1) * d]), NEG_INF)
            m = jnp.max(s_diag, axis=-1, keepdims=True)
            if i > 0:
                s_prev = _dot_nt(qc, k_ref[0:r0, c * d:(c + 1) * d])
                m = jnp.maximum(m, jnp.max(s_prev, axis=-1, keepdims=True))
            p_diag = jnp.exp2(s_diag - m)
            l = jnp.sum(p_diag, axis=-1, keepdims=True)
            if i > 0:
                p_prev = jnp.exp2(s_prev - m)
                l = l + jnp.sum(p_prev, axis=-1, keepdims=True)
            if c == 0:
                l1, w_diag = l, p_diag
                if i > 0:
                    w_prev = p_prev
            else:
                ratio = -lam * l1 / l
                w_diag = w_diag + p_diag * ratio
                if i > 0:
                    w_prev = w_prev + p_prev * ratio
        o = _dot(w_diag.astype(BF16), v_ref[r0:r0 + tq, :])
        if i > 0:
            o = o + _dot(w_prev.astype(BF16), v_ref[0:r0, :])
        o_ref[r0:r0 + tq, :] = _subln(o / l1, gs, lam_init).astype(o_ref.dtype)


def _attn_prompt(q, k, v, lam_vecs, g_subln, n_heads, lam_init, casts=()):
    b, t, width = q.shape
    hw = width // n_heads
    d = hw // 2
    tq = _tile(t, 256)
    assert tq % CHUNK == 0
    steps = b * n_heads
    if any(c[0].shape[0] % (steps * BF16_SUBLANES) or c[1] % c[2] for c in casts):
        (o,) = _attn_prompt(q, k, v, lam_vecs, g_subln, n_heads, lam_init)
        return [o] + [c[0][:, c[1]:c[1] + c[2]].astype(BF16) for c in casts]
    vec = pl.BlockSpec((1, d), lambda bi, h: (0, 0))
    seq = pl.BlockSpec((None, t, hw), lambda bi, h: (bi, 0, h))
    in_specs = [seq, seq, seq, vec, vec, vec, vec, pl.BlockSpec((1, hw), lambda bi, h: (0, 0))]
    args = [q, k, v, *lam_vecs, g_subln]
    out_specs, out_shape = [seq], [jax.ShapeDtypeStruct((b, t, width), BF16)]
    body = functools.partial(_attn_prompt_kernel, tq=tq, d=d, lam_init=lam_init)
    if casts:
        body = _with_casts(body, len(args), 1, len(casts))
    for src, c0, n in casts:
        slab = src.shape[0] // steps
        in_specs.append(pl.BlockSpec((slab, n), lambda bi, h, o=c0 // n: (bi * n_heads + h, o)))
        args.append(src)
        out_specs.append(pl.BlockSpec((slab, n), lambda bi, h: (bi * n_heads + h, 0)))
        out_shape.append(jax.ShapeDtypeStruct((src.shape[0], n), BF16))
    return pl.pallas_call(
        body,
        grid=(b, n_heads),
        in_specs=in_specs,
        out_specs=out_specs,
        out_shape=out_shape,
        compiler_params=_params("parallel", "parallel"),
        name="attn_prompt",
    )(*args)


def _attn_sample_kernel(q_ref, ck_ref, cv0_ref, cv1_ref, kn_ref, vn_ref, lq1, lk1, lq2, lk2, gs_ref, o_ref,
                        s_sc, p_sc, m_sc, l_sc, acc_sc, *, n_heads, d, tk, past, lam_init):
    kb = pl.program_id(1)
    nkb = pl.num_programs(1) - 1
    hw = 2 * d
    tq = q_ref.shape[0]
    half = n_heads // 2

    def slot(h, c):
        return (h % half) * 4 + (h // half) * 2 + c

    def rows(first_slot, n_slots=1):
        return slice(first_slot * tq, (first_slot + n_slots) * tq)

    def q_of(h, c):
        return q_ref[:, h * hw + c * d:h * hw + (c + 1) * d]

    @pl.when(kb == 0)
    def _():
        m_sc[...] = jnp.full(m_sc.shape, -jnp.inf, F32)
        l_sc[...] = jnp.zeros(l_sc.shape, F32)
        acc_sc[...] = jnp.zeros(acc_sc.shape, F32)

    def softmax_update(width):
        s = s_sc[:, :width]
        m_old = m_sc[...]
        m_new = jnp.maximum(m_old, jnp.max(s, axis=-1, keepdims=True))
        alpha = jnp.exp2(m_old - m_new)
        p = jnp.exp2(s - _rep(m_new, width))
        l_sc[...] = alpha * l_sc[...] + jnp.sum(p, axis=-1, keepdims=True)
        m_sc[...] = m_new
        p_sc[:, :width] = p.astype(p_sc.dtype)
        return alpha

    @pl.when(kb == 0)
    def _():
        rq = (past + lax.broadcasted_iota(jnp.int32, (tq, tq), 0)) // CHUNK
        rk = (past + lax.broadcasted_iota(jnp.int32, (tq, tq), 1)) // CHUNK
        for h in range(n_heads):
            for c in range(2):
                s = _dot_nt(q_of(h, c), kn_ref[:, h * hw + c * d:h * hw + (c + 1) * d])
                s_sc[rows(slot(h, c)), :tq] = jnp.where(rk <= rq, s, NEG_INF)
        alpha = softmax_update(tq)
        for h in range(n_heads):
            r = rows(slot(h, 0), 2)
            acc_sc[r, :] = _rep(alpha[r], hw) * acc_sc[r, :] + _dot(p_sc[r, :tq], vn_ref[:, h * hw:(h + 1) * hw])

    @pl.when(kb > 0)
    def _():
        odd = lax.broadcasted_iota(jnp.int32, (tq, 2 * tk), 1) % 2 == 1
        for h in range(half):
            for c in range(2):
                k_pair = ck_ref[pl.ds(2 * h + c, 2 * tk, stride=n_heads), :].astype(BF16)
                s = _dot_nt(jnp.concatenate([q_of(h, c), q_of(h + half, c)], axis=0), k_pair)
                s_sc[rows(slot(h, c)), :] = jnp.where(odd, NEG_INF, s[:tq])
                s_sc[rows(slot(h + half, c)), :] = jnp.where(odd, s[tq:], NEG_INF)
        alpha = softmax_update(2 * tk)
        for h in range(half):
            v_pair = jnp.concatenate([cv0_ref[pl.ds(h, 2 * tk, stride=half), :],
                                      cv1_ref[pl.ds(h, 2 * tk, stride=half), :]], axis=1).astype(BF16)
            r = rows(slot(h, 0), 4)
            acc_sc[r, :] = _rep(alpha[r], hw) * acc_sc[r, :] + _dot(p_sc[r, :], v_pair)

    @pl.when(kb == nkb)
    def _():
        lam = _lambda(lq1, lk1, lq2, lk2, lam_init)
        for h in range(n_heads):
            r1, r2 = rows(slot(h, 0)), rows(slot(h, 1))
            o = acc_sc[r1, :] / _rep(l_sc[r1, :], hw) - lam * (acc_sc[r2, :] / _rep(l_sc[r2, :], hw))
            o_ref[:, h * hw:(h + 1) * hw] = _subln(o, gs_ref[...], lam_init).astype(o_ref.dtype)


def _attn_sample(q, cache_k, cache_v, k_new, v_new, lam_vecs, g_subln, n_heads, lam_init):
    b, tq, width = q.shape
    hw = width // n_heads
    d = hw // 2
    assert d == STAT_LANES and n_heads % 2 == 0
    past = cache_v.shape[1] // n_heads
    tk = _tile(past, 512)
    nkb = past // tk
    vec = pl.BlockSpec((1, d), lambda bi, kb: (0, 0))
    new = pl.BlockSpec((None, tq, width), lambda bi, kb: (bi, 0, 0))
    ck_spec = pl.BlockSpec((None, tk * 2 * n_heads, d), lambda bi, kb: (bi, jnp.maximum(kb - 1, 0), 0))
    cv_specs = [pl.BlockSpec((None, tk * n_heads, d), lambda bi, kb, c=c: (bi, jnp.maximum(kb - 1, 0), c))
                for c in range(2)]
    return pl.pallas_call(
        functools.partial(_attn_sample_kernel, n_heads=n_heads, d=d, tk=tk, past=past, lam_init=lam_init),
        grid=(b, nkb + 1),
        in_specs=[new, ck_spec, *cv_specs, new, new, vec, vec, vec, vec,
                  pl.BlockSpec((1, hw), lambda bi, kb: (0, 0))],
        out_specs=new,
        out_shape=jax.ShapeDtypeStruct((b, tq, width), BF16),
        scratch_shapes=[
            pltpu.VMEM((2 * n_heads * tq, 2 * tk), F32),
            pltpu.VMEM((2 * n_heads * tq, 2 * tk), BF16),
            pltpu.VMEM((2 * n_heads * tq, STAT_LANES), F32),
            pltpu.VMEM((2 * n_heads * tq, STAT_LANES), F32),
            pltpu.VMEM((2 * n_heads * tq, hw), F32),
        ],
        compiler_params=_params("parallel", "arbitrary"),
        name="attn_sample",
    )(q, cache_k, cache_v, cache_v, k_new, v_new, *lam_vecs, g_subln)


def _pool_mix_kernel(u_ref, prev_ref, hist_ref, wp_ref, sc_ref, o_ref, ext_ref, s2_ref, s4_ref, s8_ref,
                     *, tt, pos0):
    assert POOL_WINDOWS == (2, 4, 8, 16)
    t = pl.program_id(1)
    gw = u_ref.shape[1] // len(POOL_WINDOWS)
    base = 2 * HIST_ROWS
    n = base + tt
    ext_ref[0:HIST_ROWS, :] = jnp.zeros((HIST_ROWS, u_ref.shape[1]), F32)
    ext_ref[HIST_ROWS:base, :] = jnp.where(t == 0, hist_ref[...], prev_ref[...])
    ext_ref[base:n, :] = u_ref[...]
    s2_ref[8:n, :] = ext_ref[8:n, :] + ext_ref[7:n - 1, :]
    s4_ref[16:n, :] = s2_ref[16:n, gw:] + s2_ref[14:n - 2, gw:]
    s8_ref[24:n, :] = s4_ref[24:n, gw:] + s4_ref[20:n - 4, gw:]
    wins = (s2_ref[base:n, 0:gw], s4_ref[base:n, 0:gw], s8_ref[base:n, 0:gw],
            s8_ref[base:n, gw:] + s8_ref[base - 8:n - 8, gw:])
    pos = pos0 + t * tt + lax.broadcasted_iota(jnp.int32, (tt, 1), 0)
    for g, w in enumerate(POOL_WINDOWS):
        cols = slice(g * gw, (g + 1) * gw)
        count = jnp.minimum(pos + 1, w).astype(F32)
        pooled = wins[g] / count - u_ref[:, cols]
        mixed = _dot(pooled.astype(BF16), wp_ref[g]) * sc_ref[:, cols]
        o_ref[:, cols] = mixed.astype(o_ref.dtype)


def _pool_mix(u, hist, w_pool, pool_scale, pos0):
    b, t, width = u.shape
    tt = _tile(t, 256)
    assert tt % HIST_ROWS == 0
    hist16 = jnp.pad(hist, ((0, 0), (HIST_ROWS - POOL_HIST, 0), (0, 0)))
    per = tt // HIST_ROWS
    return pl.pallas_call(
        functools.partial(_pool_mix_kernel, tt=tt, pos0=pos0),
        grid=(b, t // tt),
        in_specs=[
            pl.BlockSpec((None, tt, width), lambda bi, ti: (bi, ti, 0)),
            pl.BlockSpec((None, HIST_ROWS, width), lambda bi, ti: (bi, jnp.maximum(ti * per - 1, 0), 0)),
            pl.BlockSpec((None, HIST_ROWS, width), lambda bi, ti: (bi, 0, 0)),
            pl.BlockSpec(w_pool.shape, lambda bi, ti: (0, 0, 0)),
            pl.BlockSpec((1, width), lambda bi, ti: (0, 0)),
        ],
        out_specs=pl.BlockSpec((None, tt, width), lambda bi, ti: (bi, ti, 0)),
        out_shape=jax.ShapeDtypeStruct((b, t, width), BF16),
        scratch_shapes=[pltpu.VMEM((2 * HIST_ROWS + tt, width // 4 * groups), F32) for groups in (4, 4, 3, 2)],
        compiler_params=_params("parallel", "parallel"),
        name="pool_mix",
    )(u, u, hist16, w_pool, pool_scale)


def _trunk_layer(x, pos0, pool_hist, kv_cache, lam_init, w, wb):
    b, t, dm = x.shape
    m = b * t
    n_heads, d = w["n_heads"], w["head_dim"]
    qk_w = n_heads * 2 * d
    attn_w, pool_w = qk_w, w["w_pool_out"].shape[0]
    c_k, c_v, c_u, c_g = qk_w, 2 * qk_w, 2 * qk_w + attn_w, 2 * qk_w + attn_w + pool_w
    x2 = x.reshape(m, dm)
    w_in = w["w_in"]
    host = wb is None
    if host:
        wb = {"q": w_in[:, :qk_w].astype(BF16), "pool": w["w_pool"].astype(BF16)}

    def call(names, casts, *args):
        res = _tiled_call(*args, casts=casts if host else ())
        n_main = len(res) - (len(names) if host else 0)
        if host:
            wb.update(zip(names, res[n_main:]))
        return res[:n_main]

    xn = _rmsnorm_cast(x2, w["g_norm1"], NORM_EPS)
    (q,) = call(("k", "v", "u"), [(w_in, c_k, qk_w), (w_in, c_v, attn_w), (w_in, c_u, pool_w)],
                functools.partial(_q_body, scale=d ** -0.5 * LOG2E), [xn], [(wb["q"], 0)], [], [w["g_q"]], [BF16], qk_w,
                1024, 1024, "proj_q")
    k_f, k_b = call(("attn_out",), [(w["w_attn_out"], 0, dm)],
                    functools.partial(_k_body, groups=qk_w // LANES), [xn], [(wb["k"], 0)], [], [w["g_k"]],
                    [("rows", F32), BF16], qk_w, 1024, 512, "proj_k")
    v_f, v_b = call(("pool_out",), [(w["w_pool_out"], 0, dm)],
                    _v_body, [xn], [(wb["v"], 0)], [], [], [F32, BF16], attn_w, 1024, 1024, "proj_v")
    (u,) = _tiled_call(_u_body, [xn], [(wb["u"], 0)], [], [], [F32], pool_w, 1024, 1024, "proj_u")

    lam_vecs = w["lam_vecs"]
    gate_cast = (w_in, c_g, 2 * dm)
    if kv_cache is None:
        res = _attn_prompt(q.reshape(b, t, qk_w), k_b.reshape(b, t, qk_w), v_b.reshape(b, t, attn_w),
                           lam_vecs, w["g_subln"], n_heads, lam_init, casts=[gate_cast] if host else ())
        o = res[0]
        if host:
            wb["g"] = res[1]
    else:
        if host:
            wb["g"] = w_in[:, c_g:c_g + 2 * dm].astype(BF16)
        ck, cv = kv_cache
        o = _attn_sample(q.reshape(b, t, qk_w), ck, cv, k_b.reshape(b, t, qk_w), v_b.reshape(b, t, attn_w),
                         lam_vecs, w["g_subln"], n_heads, lam_init)
    (gates,) = call(("up", "o"), [(w["w_up"], 0, w["w_up"].shape[1]), (w["w_o"], 0, dm)],
                    _gate_body, [xn], [(wb["g"], 0)], [], [], [BF16], 2 * dm, 1024, 1024, "proj_gates")

    u3 = u.reshape(b, t, pool_w)
    yb_in = _pool_mix(u3, pool_hist, wb["pool"], w["pool_scale"], pos0)
    assert t >= POOL_HIST
    pool_state = u3[:, t - POOL_HIST:, :]

    (merged,) = _tiled_call(_merge_body, [o.reshape(m, attn_w), yb_in.reshape(m, pool_w)],
                            [(wb["attn_out"], 0), (wb["pool_out"], 0)], [(gates, 0), (gates, dm)], [],
                            [BF16], dm, 1024, 1024, "merge")
    x1, x1g, ssq = _tiled_call(_residual_norm_body, [merged], [(wb["o"], 0)], [(x2, 0)],
                               [("cols", w["g_norm2"])], [F32, BF16, ("stat", F32)], dm, 1024, 512, "out_proj")
    (act,) = call(("down",), [(w["w_down"], 0, dm)],
                  functools.partial(_relu2_norm_body, dim=dm), [x1g, ssq], [(wb["up"], 0)], [], [], [BF16],
                  w["w_up"].shape[1], 1024, 1024, "mlp_up")
    y = _down_proj(act, wb["down"], x1, 1024, 1024, 4096)
    return (y.reshape(b, t, dm), k_f.reshape(b, t, n_heads, 2, d), v_f.reshape(b, t, n_heads, 2 * d),
            pool_state), wb


def kernel(x_prompt, x_sample, cache_k, cache_v, state_pool, g_norm1, w_in, g_q, g_k, lambda_q1, lambda_k1,
           lambda_q2, lambda_k2, g_subln, w_attn_out, w_pool, pool_scale, w_pool_out, w_o, g_norm2, w_up,
           w_down):
    depth, dec_b, past, n_heads, _, d = cache_k.shape
    y_p, y_s = x_prompt, x_sample
    hist_p = jnp.zeros((x_prompt.shape[0], POOL_HIST, w_pool_out.shape[1]), F32)
    outs = [[] for _ in range(6)]
    for l in range(depth):
        lam_init = 0.8 - 0.6 * math.exp(-0.3 * l)
        row = lambda a: a[l].reshape(1, -1).astype(F32)
        w = dict(
            n_heads=n_heads, head_dim=d,
            g_norm1=g_norm1[l], g_norm2=row(g_norm2), g_q=row(g_q), g_k=row(g_k), g_subln=row(g_subln),
            lam_vecs=(row(lambda_q1), row(lambda_k1), row(lambda_q2), row(lambda_k2)),
            pool_scale=row(pool_scale),
            w_in=w_in[l], w_attn_out=w_attn_out[l], w_pool=w_pool[l], w_pool_out=w_pool_out[l], w_o=w_o[l],
            w_up=w_up[l], w_down=w_down[l],
        )
        (y_p, kp, vp, sp), wb = _trunk_layer(y_p, 0, hist_p, None, lam_init, w, None)
        kv = (cache_k[l].reshape(dec_b, past * n_heads * 2, d), cache_v[l].reshape(dec_b, past * n_heads, 2 * d))
        (y_s, ks, vs, ss), _ = _trunk_layer(y_s, past, state_pool[l], kv, lam_init, w, wb)
        for lst, val in zip(outs, (kp, vp, sp, ks, vs, ss)):
            lst.append(val)
    return (y_p, y_s) + tuple(jnp.stack(o) for o in outs)
```

```python
import functools
import math

import jax
import jax.numpy as jnp
from jax import lax
from jax.experimental import pallas as pl
from jax.experimental.pallas import tpu as pltpu

CHUNK = 64
POOL_WINDOWS = (2, 4, 8, 16)
POOL_HIST = max(POOL_WINDOWS) - 1
HIST_ROWS = 16
NORM_EPS = 1e-6
SUBLN_EPS = 1e-5
NEG_INF = -1e30
LOG2E = math.log2(math.e)
V7X_VMEM_LIMIT_BYTES = 56 * 1024 * 1024
LANES = 128
STAT_LANES = LANES
BF16_SUBLANES = 16

F32 = jnp.float32
BF16 = jnp.bfloat16


def _tile(dim, pref):
    t = min(dim, pref)
    while dim % t:
        t -= 1
    return t


def _params(*sem):
    return pltpu.CompilerParams(dimension_semantics=sem, vmem_limit_bytes=V7X_VMEM_LIMIT_BYTES)


def _dot(a, b):
    return jnp.dot(a, b, preferred_element_type=F32)


def _dot_nt(a, b):
    return lax.dot_general(a, b, (((1,), (1,)), ((), ())), preferred_element_type=F32)


def _rmsnorm_cast_kernel(x_ref, g_ref, o_ref, *, eps):
    x = x_ref[...]
    ms = jnp.mean(x * x, axis=-1, keepdims=True)
    o_ref[...] = (x * lax.rsqrt(ms + eps) * g_ref[...]).astype(o_ref.dtype)


def _rmsnorm_cast(x, g, eps):
    m, d = x.shape
    tm = _tile(m, 256)
    return pl.pallas_call(
        functools.partial(_rmsnorm_cast_kernel, eps=eps),
        grid=(m // tm,),
        in_specs=[pl.BlockSpec((tm, d), lambda i: (i, 0)), pl.BlockSpec((1, d), lambda i: (0, 0))],
        out_specs=pl.BlockSpec((tm, d), lambda i: (i, 0)),
        out_shape=jax.ShapeDtypeStruct((m, d), BF16),
        compiler_params=_params("parallel"),
        name="rmsnorm_cast",
    )(x, g.reshape(1, d).astype(F32))


def _with_casts(body, n_in, n_out, n_cast):
    def wrapped(*refs):
        main_in, cast_in = refs[:n_in], refs[n_in:n_in + n_cast]
        outs = refs[n_in + n_cast:]
        body(*main_in, *outs[:n_out])
        for src, dst in zip(cast_in, outs[n_out:]):
            dst[...] = src[...].astype(dst.dtype)
    return wrapped


def _tiled_call(body, rows, weights, tiles, vecs, out_dtypes, n_cols, tm, tn, name, casts=()):
    m = rows[0].shape[0]
    tm = _tile(m, tm)
    tn = _tile(n_cols, tn)
    steps = (m // tm) * (n_cols // tn)
    nj = n_cols // tn
    hosted = [c for c in casts if c[0].shape[0] % (steps * BF16_SUBLANES) == 0 and c[1] % c[2] == 0]
    if len(hosted) != len(casts):
        res = _tiled_call(body, rows, weights, tiles, vecs, out_dtypes, n_cols, tm, tn, name)
        return list(res) + [c[0][:, c[1]:c[1] + c[2]].astype(BF16) for c in casts]
    in_specs, args = [], []
    for a in rows:
        in_specs.append(pl.BlockSpec((tm, a.shape[1]), lambda i, j: (i, 0)))
        args.append(a)
    for w, c0 in weights:
        assert c0 % tn == 0
        in_specs.append(pl.BlockSpec((w.shape[0], tn), lambda i, j, o=c0 // tn: (0, j + o)))
        args.append(w)
    for t, c0 in tiles:
        assert c0 % tn == 0
        in_specs.append(pl.BlockSpec((tm, tn), lambda i, j, o=c0 // tn: (i, j + o)))
        args.append(t)
    for v in vecs:
        if isinstance(v, tuple):
            in_specs.append(pl.BlockSpec((1, tn), lambda i, j: (0, j)))
            args.append(v[1])
        else:
            in_specs.append(pl.BlockSpec(v.shape, lambda i, j: (0, 0)))
            args.append(v)
    n_in = len(args)
    out_specs, out_shape, j_sem = [], [], "parallel"
    for dt in out_dtypes:
        if not isinstance(dt, tuple):
            out_specs.append(pl.BlockSpec((tm, tn), lambda i, j: (i, j)))
            out_shape.append(jax.ShapeDtypeStruct((m, n_cols), dt))
        elif dt[0] == "rows":
            groups = n_cols // LANES
            out_specs.append(pl.BlockSpec((tm * groups, LANES), lambda i, j: (i, 0)))
            out_shape.append(jax.ShapeDtypeStruct((m * groups, LANES), dt[1]))
            j_sem = "arbitrary"
        else:
            out_specs.append(pl.BlockSpec((tm, LANES), lambda i, j: (i, j)))
            out_shape.append(jax.ShapeDtypeStruct((m, nj * LANES), dt[1]))
    for src, c0, n in casts:
        slab = src.shape[0] // steps
        in_specs.append(pl.BlockSpec((slab, n), lambda i, j, o=c0 // n: (i * nj + j, o)))
        args.append(src)
        out_specs.append(pl.BlockSpec((slab, n), lambda i, j: (i * nj + j, 0)))
        out_shape.append(jax.ShapeDtypeStruct((src.shape[0], n), BF16))
    if casts:
        body = _with_casts(body, n_in, len(out_dtypes), len(casts))
    return pl.pallas_call(
        body,
        grid=(m // tm, n_cols // tn),
        in_specs=in_specs,
        out_specs=out_specs,
        out_shape=out_shape,
        compiler_params=_params("parallel", j_sem),
        name=name,
    )(*args)


def _head_rmsnorm_store(acc, g, eps, scale, out_refs):
    group = g.shape[1]
    for c in range(acc.shape[1] // group):
        sl = slice(c * group, (c + 1) * group)
        blk = acc[:, sl]
        ms = jnp.mean(blk * blk, axis=-1, keepdims=True)
        y = blk * lax.rsqrt(ms + eps) * g
        for ref, s in zip(out_refs, scale):
            ref[:, sl] = (y if s == 1.0 else y * s).astype(ref.dtype)


def _q_body(a_ref, w_ref, g_ref, q_ref, *, scale):
    _head_rmsnorm_store(_dot(a_ref[...], w_ref[...]), g_ref[...], NORM_EPS, (scale,), (q_ref,))


def _k_body(a_ref, w_ref, g_ref, kf_ref, kb_ref, *, groups):
    acc = _dot(a_ref[...], w_ref[...])
    g = g_ref[...]
    tm, tn = acc.shape
    first = pl.program_id(1) * (tn // LANES)
    for c in range(tn // LANES):
        blk = acc[:, c * LANES:(c + 1) * LANES]
        ms = jnp.mean(blk * blk, axis=-1, keepdims=True)
        y = blk * lax.rsqrt(ms + NORM_EPS) * g
        kf_ref[pl.ds(first + c, tm, stride=groups), :] = y
        kb_ref[:, c * LANES:(c + 1) * LANES] = y.astype(kb_ref.dtype)


def _v_body(a_ref, w_ref, vf_ref, vb_ref):
    acc = _dot(a_ref[...], w_ref[...])
    vf_ref[...] = acc
    vb_ref[...] = acc.astype(vb_ref.dtype)


def _u_body(a_ref, w_ref, u_ref):
    u_ref[...] = _dot(a_ref[...], w_ref[...])


def _gate_body(a_ref, w_ref, g_ref):
    acc = _dot(a_ref[...], w_ref[...])
    g_ref[...] = (0.5 * jnp.tanh(0.5 * acc) + 0.5).astype(g_ref.dtype)


def _merge_body(o_ref, yb_ref, wa_ref, wp_ref, ga_ref, gb_ref, out_ref):
    ya = _dot(o_ref[...], wa_ref[...])
    yb = _dot(yb_ref[...], wp_ref[...])
    out_ref[...] = (ga_ref[...].astype(F32) * ya + gb_ref[...].astype(F32) * yb).astype(out_ref.dtype)


def _residual_norm_body(a_ref, w_ref, x_ref, g_ref, out_ref, xg_ref, ssq_ref):
    x1 = x_ref[...] + _dot(a_ref[...], w_ref[...])
    out_ref[...] = x1
    xg_ref[...] = (x1 * g_ref[...]).astype(xg_ref.dtype)
    ssq_ref[...] = jnp.broadcast_to(jnp.sum(x1 * x1, axis=-1, keepdims=True), ssq_ref.shape)


def _relu2_norm_body(a_ref, ssq_ref, w_ref, out_ref, *, dim):
    ssq = ssq_ref[...]
    tot = ssq[:, :LANES]
    for c in range(1, ssq.shape[1] // LANES):
        tot = tot + ssq[:, c * LANES:(c + 1) * LANES]
    r = lax.rsqrt(tot * (1.0 / dim) + NORM_EPS)
    acc = _dot(a_ref[...], w_ref[...])
    h = jnp.maximum(acc * _rep(r, acc.shape[1]), 0.0)
    out_ref[...] = (h * h).astype(out_ref.dtype)


def _down_body(a_ref, w_ref, x_ref, out_ref):
    @pl.when(pl.program_id(2) == 0)
    def _():
        out_ref[...] = x_ref[...]

    out_ref[...] += _dot(a_ref[...], w_ref[...])


def _down_proj(a, w, x, tm, tn, tk):
    m, kdim = a.shape
    n = w.shape[1]
    tm, tn, tk = _tile(m, tm), _tile(n, tn), _tile(kdim, tk)
    return pl.pallas_call(
        _down_body,
        grid=(m // tm, n // tn, kdim // tk),
        in_specs=[
            pl.BlockSpec((tm, tk), lambda i, j, k: (i, k)),
            pl.BlockSpec((tk, tn), lambda i, j, k: (k, j)),
            pl.BlockSpec((tm, tn), lambda i, j, k: (i, j)),
        ],
        out_specs=pl.BlockSpec((tm, tn), lambda i, j, k: (i, j)),
        out_shape=jax.ShapeDtypeStruct((m, n), F32),
        compiler_params=_params("parallel", "parallel", "arbitrary"),
        name="down_proj",
    )(a, w, x)


def _rep(x, width):
    if width <= STAT_LANES:
        return x[:, :width]
    return jnp.concatenate([x] * (width // STAT_LANES), axis=1)


def _lambda(lq1, lk1, lq2, lk2, lam_init):
    return (jnp.exp(jnp.sum(lq1[...] * lk1[...], keepdims=True))
            - jnp.exp(jnp.sum(lq2[...] * lk2[...], keepdims=True)) + lam_init)


def _subln(o, gs, lam_init):
    ms = jnp.mean(o * o, axis=-1, keepdims=True)
    return o * lax.rsqrt(ms + SUBLN_EPS) * gs * (1.0 - lam_init)


def _attn_prompt_kernel(q_ref, k_ref, v_ref, lq1, lk1, lq2, lk2, gs_ref, o_ref, *, tq, d, lam_init):
    t = q_ref.shape[0]
    lam = _lambda(lq1, lk1, lq2, lk2, lam_init)
    gs = gs_ref[...]
    rq = lax.broadcasted_iota(jnp.int32, (tq, tq), 0) // CHUNK
    rk = lax.broadcasted_iota(jnp.int32, (tq, tq), 1) // CHUNK
    diag_mask = rk <= rq
    for i in range(t // tq):
        r0 = i * tq
        w_diag, w_prev = None, None
        for c in range(2):
            qc = q_ref[r0:r0 + tq, c * d:(c + 1) * d]
            s_diag = jnp.where(diag_mask, _dot_nt(qc, k_ref[r0:r0 + tq, c * d:(c + 1) * d]), NEG_INF)
            m = jnp.max(s_diag, axis=-1, keepdims=True)
            if i > 0:
                s_prev = _dot_nt(qc, k_ref[0:r0, c * d:(c + 1) * d])
                m = jnp.maximum(m, jnp.max(s_prev, axis=-1, keepdims=True))
            p_diag = jnp.exp2(s_diag - m)
            l = jnp.sum(p_diag, axis=-1, keepdims=True)
            if i > 0:
                p_prev = jnp.exp2(s_prev - m)
                l = l + jnp.sum(p_prev, axis=-1, keepdims=True)
            if c == 0:
                l1, w_diag = l, p_diag
                if i > 0:
                    w_prev = p_prev
            else:
                ratio = -lam * l1 / l
                w_diag = w_diag + p_diag * ratio
                if i > 0:
                    w_prev = w_prev + p_prev * ratio
        o = _dot(w_diag.astype(BF16), v_ref[r0:r0 + tq, :])
        if i > 0:
            o = o + _dot(w_prev.astype(BF16), v_ref[0:r0, :])
        o_ref[r0:r0 + tq, :] = _subln(o / l1, gs, lam_init).astype(o_ref.dtype)


def _attn_prompt(q, k, v, lam_vecs, g_subln, n_heads, lam_init, casts=()):
    b, t, width = q.shape
    hw = width // n_heads
    d = hw // 2
    tq = _tile(t, 256)
    assert tq % CHUNK == 0
    steps = b * n_heads
    if any(c[0].shape[0] % (steps * BF16_SUBLANES) or c[1] % c[2] for c in casts):
        (o,) = _attn_prompt(q, k, v, lam_vecs, g_subln, n_heads, lam_init)
        return [o] + [c[0][:, c[1]:c[1] + c[2]].astype(BF16) for c in casts]
    vec = pl.BlockSpec((1, d), lambda bi, h: (0, 0))
    seq = pl.BlockSpec((None, t, hw), lambda bi, h: (bi, 0, h))
    in_specs = [seq, seq, seq, vec, vec, vec, vec, pl.BlockSpec((1, hw), lambda bi, h: (0, 0))]
    args = [q, k, v, *lam_vecs, g_subln]
    out_specs, out_shape = [seq], [jax.ShapeDtypeStruct((b, t, width), BF16)]
    body = functools.partial(_attn_prompt_kernel, tq=tq, d=d, lam_init=lam_init)
    if casts:
        body = _with_casts(body, len(args), 1, len(casts))
    for src, c0, n in casts:
        slab = src.shape[0] // steps
        in_specs.append(pl.BlockSpec((slab, n), lambda bi, h, o=c0 // n: (bi * n_heads + h, o)))
        args.append(src)
        out_specs.append(pl.BlockSpec((slab, n), lambda bi, h: (bi * n_heads + h, 0)))
        out_shape.append(jax.ShapeDtypeStruct((src.shape[0], n), BF16))
    return pl.pallas_call(
        body,
        grid=(b, n_heads),
        in_specs=in_specs,
        out_specs=out_specs,
        out_shape=out_shape,
        compiler_params=_params("parallel", "parallel"),
        name="attn_prompt",
    )(*args)


def _attn_sample_kernel(q_ref, ck_ref, cv0_ref, cv1_ref, kn_ref, vn_ref, lq1, lk1, lq2, lk2, gs_ref, o_ref,
                        s_sc, p_sc, m_sc, l_sc, acc_sc, *, n_heads, d, tk, past, lam_init):
    kb = pl.program_id(1)
    nkb = pl.num_programs(1) - 1
    hw = 2 * d
    tq = q_ref.shape[0]
    half = n_heads // 2

    def slot(h, c):
        return (h % half) * 4 + (h // half) * 2 + c

    def rows(first_slot, n_slots=1):
        return slice(first_slot * tq, (first_slot + n_slots) * tq)

    def q_of(h, c):
        return q_ref[:, h * hw + c * d:h * hw + (c + 1) * d]

    @pl.when(kb == 0)
    def _():
        m_sc[...] = jnp.full(m_sc.shape, -jnp.inf, F32)
        l_sc[...] = jnp.zeros(l_sc.shape, F32)
        acc_sc[...] = jnp.zeros(acc_sc.shape, F32)

    def softmax_update(width):
        s = s_sc[:, :width]
        m_old = m_sc[...]
        m_new = jnp.maximum(m_old, jnp.max(s, axis=-1, keepdims=True))
        alpha = jnp.exp2(m_old - m_new)
        p = jnp.exp2(s - _rep(m_new, width))
        l_sc[...] = alpha * l_sc[...] + jnp.sum(p, axis=-1, keepdims=True)
        m_sc[...] = m_new
        p_sc[:, :width] = p.astype(p_sc.dtype)
        return alpha

    @pl.when(kb == 0)
    def _():
        rq = (past + lax.broadcasted_iota(jnp.int32, (tq, tq), 0)) // CHUNK
        rk = (past + lax.broadcasted_iota(jnp.int32, (tq, tq), 1)) // CHUNK
        for h in range(n_heads):
            for c in range(2):
                s = _dot_nt(q_of(h, c), kn_ref[:, h * hw + c * d:h * hw + (c + 1) * d])
                s_sc[rows(slot(h, c)), :tq] = jnp.where(rk <= rq, s, NEG_INF)
        alpha = softmax_update(tq)
        for h in range(n_heads):
            r = rows(slot(h, 0), 2)
            acc_sc[r, :] = _rep(alpha[r], hw) * acc_sc[r, :] + _dot(p_sc[r, :tq], vn_ref[:, h * hw:(h + 1) * hw])

    @pl.when(kb > 0)
    def _():
        odd = lax.broadcasted_iota(jnp.int32, (tq, 2 * tk), 1) % 2 == 1
        for h in range(half):
            for c in range(2):
                k_pair = ck_ref[pl.ds(2 * h + c, 2 * tk, stride=n_heads), :].astype(BF16)
                s = _dot_nt(jnp.concatenate([q_of(h, c), q_of(h + half, c)], axis=0), k_pair)
                s_sc[rows(slot(h, c)), :] = jnp.where(odd, NEG_INF, s[:tq])
                s_sc[rows(slot(h + half, c)), :] = jnp.where(odd, s[tq:], NEG_INF)
        alpha = softmax_update(2 * tk)
        for h in range(half):
            v_pair = jnp.concatenate([cv0_ref[pl.ds(h, 2 * tk, stride=half), :],
                                      cv1_ref[pl.ds(h, 2 * tk, stride=half), :]], axis=1).astype(BF16)
            r = rows(slot(h, 0), 4)
            acc_sc[r, :] = _rep(alpha[r], hw) * acc_sc[r, :] + _dot(p_sc[r, :], v_pair)

    @pl.when(kb == nkb)
    def _():
        lam = _lambda(lq1, lk1, lq2, lk2, lam_init)
        for h in range(n_heads):
            r1, r2 = rows(slot(h, 0)), rows(slot(h, 1))
            o = acc_sc[r1, :] / _rep(l_sc[r1, :], hw) - lam * (acc_sc[r2, :] / _rep(l_sc[r2, :], hw))
            o_ref[:, h * hw:(h + 1) * hw] = _subln(o, gs_ref[...], lam_init).astype(o_ref.dtype)


def _attn_sample(q, cache_k, cache_v, k_new, v_new, lam_vecs, g_subln, n_heads, lam_init):
    b, tq, width = q.shape
    hw = width // n_heads
    d = hw // 2
    assert d == STAT_LANES and n_heads % 2 == 0
    past = cache_v.shape[1] // n_heads
    tk = _tile(past, 1024)
    nkb = past // tk
    vec = pl.BlockSpec((1, d), lambda bi, kb: (0, 0))
    new = pl.BlockSpec((None, tq, width), lambda bi, kb: (bi, 0, 0))
    ck_spec = pl.BlockSpec((None, tk * 2 * n_heads, d), lambda bi, kb: (bi, jnp.maximum(kb - 1, 0), 0))
    cv_specs = [pl.BlockSpec((None, tk * n_heads, d), lambda bi, kb, c=c: (bi, jnp.maximum(kb - 1, 0), c))
                for c in range(2)]
    return pl.pallas_call(
        functools.partial(_attn_sample_kernel, n_heads=n_heads, d=d, tk=tk, past=past, lam_init=lam_init),
        grid=(b, nkb + 1),
        in_specs=[new, ck_spec, *cv_specs, new, new, vec, vec, vec, vec,
                  pl.BlockSpec((1, hw), lambda bi, kb: (0, 0))],
        out_specs=new,
        out_shape=jax.ShapeDtypeStruct((b, tq, width), BF16),
        scratch_shapes=[
            pltpu.VMEM((2 * n_heads * tq, 2 * tk), F32),
            pltpu.VMEM((2 * n_heads * tq, 2 * tk), BF16),
            pltpu.VMEM((2 * n_heads * tq, STAT_LANES), F32),
            pltpu.VMEM((2 * n_heads * tq, STAT_LANES), F32),
            pltpu.VMEM((2 * n_heads * tq, hw), F32),
        ],
        compiler_params=_params("parallel", "arbitrary"),
        name="attn_sample",
    )(q, cache_k, cache_v, cache_v, k_new, v_new, *lam_vecs, g_subln)


def _pool_mix_kernel(u_ref, prev_ref, hist_ref, wp_ref, sc_ref, o_ref, ext_ref, s2_ref, s4_ref, s8_ref,
                     *, tt, pos0):
    assert POOL_WINDOWS == (2, 4, 8, 16)
    t = pl.program_id(1)
    gw = u_ref.shape[1] // len(POOL_WINDOWS)
    base = 2 * HIST_ROWS
    n = base + tt
    ext_ref[0:HIST_ROWS, :] = jnp.zeros((HIST_ROWS, u_ref.shape[1]), F32)
    ext_ref[HIST_ROWS:base, :] = jnp.where(t == 0, hist_ref[...], prev_ref[...])
    ext_ref[base:n, :] = u_ref[...]
    s2_ref[8:n, :] = ext_ref[8:n, :] + ext_ref[7:n - 1, :]
    s4_ref[16:n, :] = s2_ref[16:n, gw:] + s2_ref[14:n - 2, gw:]
    s8_ref[24:n, :] = s4_ref[24:n, gw:] + s4_ref[20:n - 4, gw:]
    wins = (s2_ref[base:n, 0:gw], s4_ref[base:n, 0:gw], s8_ref[base:n, 0:gw],
            s8_ref[base:n, gw:] + s8_ref[base - 8:n - 8, gw:])
    pos = pos0 + t * tt + lax.broadcasted_iota(jnp.int32, (tt, 1), 0)
    for g, w in enumerate(POOL_WINDOWS):
        cols = slice(g * gw, (g + 1) * gw)
        count = jnp.minimum(pos + 1, w).astype(F32)
        pooled = wins[g] / count - u_ref[:, cols]
        mixed = _dot(pooled.astype(BF16), wp_ref[g]) * sc_ref[:, cols]
        o_ref[:, cols] = mixed.astype(o_ref.dtype)


def _pool_mix(u, hist, w_pool, pool_scale, pos0):
    b, t, width = u.shape
    tt = _tile(t, 256)
    assert tt % HIST_ROWS == 0
    hist16 = jnp.pad(hist, ((0, 0), (HIST_ROWS - POOL_HIST, 0), (0, 0)))
    per = tt // HIST_ROWS
    return pl.pallas_call(
        functools.partial(_pool_mix_kernel, tt=tt, pos0=pos0),
        grid=(b, t // tt),
        in_specs=[
            pl.BlockSpec((None, tt, width), lambda bi, ti: (bi, ti, 0)),
            pl.BlockSpec((None, HIST_ROWS, width), lambda bi, ti: (bi, jnp.maximum(ti * per - 1, 0), 0)),
            pl.BlockSpec((None, HIST_ROWS, width), lambda bi, ti: (bi, 0, 0)),
            pl.BlockSpec(w_pool.shape, lambda bi, ti: (0, 0, 0)),
            pl.BlockSpec((1, width), lambda bi, ti: (0, 0)),
        ],
        out_specs=pl.BlockSpec((None, tt, width), lambda bi, ti: (bi, ti, 0)),
        out_shape=jax.ShapeDtypeStruct((b, t, width), BF16),
        scratch_shapes=[pltpu.VMEM((2 * HIST_ROWS + tt, width // 4 * groups), F32) for groups in (4, 4, 3, 2)],
        compiler_params=_params("parallel", "parallel"),
        name="pool_mix",
    )(u, u, hist16, w_pool, pool_scale)


def _trunk_layer(x, pos0, pool_hist, kv_cache, lam_init, w, wb):
    b, t, dm = x.shape
    m = b * t
    n_heads, d = w["n_heads"], w["head_dim"]
    qk_w = n_heads * 2 * d
    attn_w, pool_w = qk_w, w["w_pool_out"].shape[0]
    c_k, c_v, c_u, c_g = qk_w, 2 * qk_w, 2 * qk_w + attn_w, 2 * qk_w + attn_w + pool_w
    x2 = x.reshape(m, dm)
    w_in = w["w_in"]
    host = wb is None
    if host:
        wb = {"q": w_in[:, :qk_w].astype(BF16), "pool": w["w_pool"].astype(BF16)}

    def call(names, casts, *args):
        res = _tiled_call(*args, casts=casts if host else ())
        n_main = len(res) - (len(names) if host else 0)
        if host:
            wb.update(zip(names, res[n_main:]))
        return res[:n_main]

    xn = _rmsnorm_cast(x2, w["g_norm1"], NORM_EPS)
    (q,) = call(("k", "v", "u"), [(w_in, c_k, qk_w), (w_in, c_v, attn_w), (w_in, c_u, pool_w)],
                functools.partial(_q_body, scale=d ** -0.5 * LOG2E), [xn], [(wb["q"], 0)], [], [w["g_q"]], [BF16], qk_w,
                1024, 1024, "proj_q")
    k_f, k_b = call(("attn_out",), [(w["w_attn_out"], 0, dm)],
                    functools.partial(_k_body, groups=qk_w // LANES), [xn], [(wb["k"], 0)], [], [w["g_k"]],
                    [("rows", F32), BF16], qk_w, 1024, 512, "proj_k")
    v_f, v_b = call(("pool_out",), [(w["w_pool_out"], 0, dm)],
                    _v_body, [xn], [(wb["v"], 0)], [], [], [F32, BF16], attn_w, 1024, 1024, "proj_v")
    (u,) = _tiled_call(_u_body, [xn], [(wb["u"], 0)], [], [], [F32], pool_w, 1024, 1024, "proj_u")

    lam_vecs = w["lam_vecs"]
    gate_cast = (w_in, c_g, 2 * dm)
    if kv_cache is None:
        res = _attn_prompt(q.reshape(b, t, qk_w), k_b.reshape(b, t, qk_w), v_b.reshape(b, t, attn_w),
                           lam_vecs, w["g_subln"], n_heads, lam_init, casts=[gate_cast] if host else ())
        o = res[0]
        if host:
            wb["g"] = res[1]
    else:
        if host:
            wb["g"] = w_in[:, c_g:c_g + 2 * dm].astype(BF16)
        ck, cv = kv_cache
        o = _attn_sample(q.reshape(b, t, qk_w), ck, cv, k_b.reshape(b, t, qk_w), v_b.reshape(b, t, attn_w),
                         lam_vecs, w["g_subln"], n_heads, lam_init)
    (gates,) = call(("up", "o"), [(w["w_up"], 0, w["w_up"].shape[1]), (w["w_o"], 0, dm)],
                    _gate_body, [xn], [(wb["g"], 0)], [], [], [BF16], 2 * dm, 1024, 1024, "proj_gates")

    u3 = u.reshape(b, t, pool_w)
    yb_in = _pool_mix(u3, pool_hist, wb["pool"], w["pool_scale"], pos0)
    assert t >= POOL_HIST
    pool_state = u3[:, t - POOL_HIST:, :]

    (merged,) = _tiled_call(_merge_body, [o.reshape(m, attn_w), yb_in.reshape(m, pool_w)],
                            [(wb["attn_out"], 0), (wb["pool_out"], 0)], [(gates, 0), (gates, dm)], [],
                            [BF16], dm, 1024, 1024, "merge")
    x1, x1g, ssq = _tiled_call(_residual_norm_body, [merged], [(wb["o"], 0)], [(x2, 0)],
                               [("cols", w["g_norm2"])], [F32, BF16, ("stat", F32)], dm, 1024, 512, "out_proj")
    (act,) = call(("down",), [(w["w_down"], 0, dm)],
                  functools.partial(_relu2_norm_body, dim=dm), [x1g, ssq], [(wb["up"], 0)], [], [], [BF16],
                  w["w_up"].shape[1], 1024, 1024, "mlp_up")
    y = _down_proj(act, wb["down"], x1, 1024, 1024, 4096)
    return (y.reshape(b, t, dm), k_f.reshape(b, t, n_heads, 2, d), v_f.reshape(b, t, n_heads, 2 * d),
            pool_state), wb


def kernel(x_prompt, x_sample, cache_k, cache_v, state_pool, g_norm1, w_in, g_q, g_k, lambda_q1, lambda_k1,
           lambda_q2, lambda_k2, g_subln, w_attn_out, w_pool, pool_scale, w_pool_out, w_o, g_norm2, w_up,
           w_down):
    depth, dec_b, past, n_heads, _, d = cache_k.shape
    y_p, y_s = x_prompt, x_sample
    hist_p = jnp.zeros((x_prompt.shape[0], POOL_HIST, w_pool_out.shape[1]), F32)
    outs = [[] for _ in range(6)]
    for l in range(depth):
        lam_init = 0.8 - 0.6 * math.exp(-0.3 * l)
        row = lambda a: a[l].reshape(1, -1).astype(F32)
        w = dict(
            n_heads=n_heads, head_dim=d,
            g_norm1=g_norm1[l], g_norm2=row(g_norm2), g_q=row(g_q), g_k=row(g_k), g_subln=row(g_subln),
            lam_vecs=(row(lambda_q1), row(lambda_k1), row(lambda_q2), row(lambda_k2)),
            pool_scale=row(pool_scale),
            w_in=w_in[l], w_attn_out=w_attn_out[l], w_pool=w_pool[l], w_pool_out=w_pool_out[l], w_o=w_o[l],
            w_up=w_up[l], w_down=w_down[l],
        )
        (y_p, kp, vp, sp), wb = _trunk_layer(y_p, 0, hist_p, None, lam_init, w, None)
        kv = (cache_k[l].reshape(dec_b, past * n_heads * 2, d), cache_v[l].reshape(dec_b, past * n_heads, 2 * d))
        (y_s, ks, vs, ss), _ = _trunk_layer(y_s, past, state_pool[l], kv, lam_init, w, wb)
        for lst, val in zip(outs, (kp, vp, sp, ks, vs, ss)):
            lst.append(val)
    return (y_p, y_s) + tuple(jnp.stack(o) for o in outs)
```
